```python
import jax, jax.numpy as jnp
from jax import lax
import numpy as np

D_MODEL = 1024
BATCH = 2
SEQ = 8192
DEPTH = 1

GLA_HEADS = 4
GLA_DK = 128
GLA_DV = 256
GLA_KEY = GLA_HEADS * GLA_DK
GLA_VAL = GLA_HEADS * GLA_DV
GLA_GATE_RANK = 16
GLA_GATE_NORM = 16.0
GLA_CHUNK = 64
SSM_INNER = 2 * D_MODEL
SSM_HEADDIM = 64
SSM_HEADS = SSM_INNER // SSM_HEADDIM
SSM_GROUPS = 4
SSM_HPG = SSM_HEADS // SSM_GROUPS
SSM_STATE = 128
SSM_CONV = 4
SSM_CHUNK = 128
SSM_BC = SSM_GROUPS * SSM_STATE
SSM_XBC = SSM_INNER + 2 * SSM_BC
FFN_HIDDEN = 2816
FFN_CONV = 3
PLE_DIM = 256
EPS = 1e-6

IN_SPLITS = (GLA_KEY, GLA_KEY, GLA_VAL, GLA_VAL, GLA_GATE_RANK,
             SSM_INNER, SSM_XBC, SSM_HEADS, D_MODEL, D_MODEL)
IN_WIDTH = sum(IN_SPLITS)

kernel_name = "hybrid_gla_ssd_gated_merge_block"


def rms_norm(x, gain):
    xf = x.astype(jnp.float32)
    y = xf * lax.rsqrt(jnp.mean(xf * xf, axis=-1, keepdims=True) + EPS)
    return (y * gain.astype(jnp.float32)).astype(x.dtype)


def split_cols(a, sizes):
    idx, s = [], 0
    for n in sizes[:-1]:
        s += n
        idx.append(s)
    return jnp.split(a, idx, axis=-1)


def causal_dwconv(x, w):
    k_width, ch = w.shape
    return lax.conv_general_dilated(
        x, w.astype(x.dtype)[:, None, :], window_strides=(1,),
        padding=[(k_width - 1, 0)], dimension_numbers=('NWC', 'WIO', 'NWC'),
        feature_group_count=ch)


def gla_mixer(q, k, v, g_out, a_lr, w_gate, b_gate, head_norm):
    f32 = jnp.float32
    bsz, seq, _ = q.shape
    H, dk, dv, C = GLA_HEADS, GLA_DK, GLA_DV, GLA_CHUNK
    n = seq // C
    g_log = jax.nn.log_sigmoid((a_lr @ w_gate + b_gate).astype(f32)) / GLA_GATE_NORM
    qf = q.astype(f32).reshape(bsz, n, C, H, dk) * (dk ** -0.5)
    kf = k.astype(f32).reshape(bsz, n, C, H, dk)
    vf = v.astype(f32).reshape(bsz, n, C, H, dv)
    b = jnp.cumsum(g_log.reshape(bsz, n, C, H, dk), axis=2)
    b_last = b[:, :, -1]
    q_t = qf * jnp.exp(b)
    k_t = kf * jnp.exp(-b)
    causal = jnp.tril(jnp.ones((C, C), dtype=bool))
    att = jnp.where(causal, jnp.einsum('bnihd,bnjhd->bnhij', q_t, k_t), 0.0)
    o_intra = jnp.einsum('bnhij,bnjhv->bnihv', att, vf)
    u = jnp.einsum('bnchd,bnchv->nbhdv', kf * jnp.exp(b_last[:, :, None] - b), vf)
    decay = jnp.exp(b_last).transpose(1, 0, 2, 3)

    def step(state, inp):
        dec, u_n = inp
        return dec[..., None] * state + u_n, state

    _, s_prev = lax.scan(step, jnp.zeros((bsz, H, dk, dv), f32), (decay, u))
    o_inter = jnp.einsum('bnihd,nbhdv->bnihv', q_t, s_prev)
    o = (o_intra + o_inter).reshape(bsz, seq, H, dv)
    o = o * lax.rsqrt(jnp.mean(o * o, axis=-1, keepdims=True) + EPS) * head_norm.astype(f32)
    o = o.reshape(bsz, seq, GLA_VAL) * jax.nn.silu(g_out.astype(f32))
    return o.astype(q.dtype)


def ssd_mixer(z, xbc, dt_raw, w_conv, b_conv, dt_bias, a_log, d_skip, out_norm):
    f32 = jnp.float32
    bsz, seq, _ = xbc.shape
    G, J, P, N, L = SSM_GROUPS, SSM_HPG, SSM_HEADDIM, SSM_STATE, SSM_CHUNK
    nc = seq // L
    xbc = jax.nn.silu(causal_dwconv(xbc, w_conv) + b_conv)
    xs, bm, cm = jnp.split(xbc, [SSM_INNER, SSM_INNER + SSM_BC], axis=-1)
    dt = jax.nn.softplus(dt_raw.astype(f32) + dt_bias.astype(f32))
    a_head = -jnp.exp(a_log.astype(f32)).reshape(G, J)
    X = xs.astype(f32).reshape(bsz, nc, L, G, J, P)
    dtc = dt.reshape(bsz, nc, L, G, J)
    Bc = bm.astype(f32).reshape(bsz, nc, L, G, N)
    Cc = cm.astype(f32).reshape(bsz, nc, L, G, N)
    a_cs = jnp.cumsum(dtc * a_head, axis=2)
    Xdt = X * dtc[..., None]
    seg = a_cs[:, :, :, None] - a_cs[:, :, None]
    causal = jnp.tril(jnp.ones((L, L), dtype=bool))[:, :, None, None]
    l_dec = jnp.exp(jnp.where(causal, seg, -jnp.inf))
    scores = jnp.einsum('bclgn,bcsgn->bclsg', Cc, Bc)
    y_diag = jnp.einsum('bclsgj,bcsgjp->bclgjp', scores[..., None] * l_dec, Xdt)
    decay_states = jnp.exp(a_cs[:, :, -1:] - a_cs)
    states = jnp.einsum('bclgn,bclgjp->cbgjpn', Bc, Xdt * decay_states[..., None])
    chunk_decay = jnp.exp(a_cs[:, :, -1]).transpose(1, 0, 2, 3)

    def step(state, inp):
        dec, s_n = inp
        return dec[..., None, None] * state + s_n, state

    _, s_prev = lax.scan(step, jnp.zeros((bsz, G, J, P, N), f32), (chunk_decay, states))
    y_off = jnp.einsum('bclgn,cbgjpn->bclgjp', Cc, s_prev) * jnp.exp(a_cs)[..., None]
    y = y_diag + y_off + X * d_skip.astype(f32).reshape(G, J)[..., None]
    y = y.reshape(bsz, seq, SSM_INNER) * jax.nn.silu(z.astype(f32))
    yg = y.reshape(bsz, seq, G, SSM_INNER // G)
    yg = yg * lax.rsqrt(jnp.mean(yg * yg, axis=-1, keepdims=True) + EPS)
    y = yg.reshape(bsz, seq, SSM_INNER) * out_norm.astype(f32)
    return y.astype(z.dtype)


def setup_inputs(seed: int = 0) -> dict:
    key = jax.random.key(seed)
    ks = jax.random.split(key, 26)

    def nrm(k, shape, scale):
        return jax.random.normal(k, shape, jnp.float32) * scale

    def gain(k, width):
        return 1.0 + nrm(k, (DEPTH, width), 0.02)

    dt0 = jnp.exp(jax.random.uniform(ks[8], (DEPTH, SSM_HEADS), jnp.float32,
                                     np.log(1e-3), np.log(1e-1)))
    return {
        "x": nrm(ks[0], (BATCH, SEQ, D_MODEL), 1.0),
        "p": nrm(ks[1], (DEPTH, BATCH, SEQ, PLE_DIM), 1.0),
        "mixer_norm": gain(ks[2], D_MODEL),
        "w_in": nrm(ks[3], (DEPTH, D_MODEL, IN_WIDTH), D_MODEL ** -0.5),
        "w_gla_gate": nrm(ks[4], (DEPTH, GLA_GATE_RANK, GLA_KEY), GLA_GATE_RANK ** -0.5),
        "b_gla_gate": nrm(ks[5], (DEPTH, GLA_KEY), 0.1),
        "gla_norm": gain(ks[6], GLA_DV),
        "w_ssm_conv": nrm(ks[7], (DEPTH, SSM_CONV, SSM_XBC), SSM_CONV ** -0.5),
        "b_ssm_conv": nrm(ks[9], (DEPTH, SSM_XBC), 0.02),
        "dt_bias": dt0 + jnp.log(-jnp.expm1(-dt0)),
        "a_log": jnp.log(jax.random.uniform(ks[10], (DEPTH, SSM_HEADS), jnp.float32, 1.0, 16.0)),
        "d_skip": 1.0 + nrm(ks[11], (DEPTH, SSM_HEADS), 0.02),
        "ssm_norm": gain(ks[12], SSM_INNER),
        "w_branch_a": nrm(ks[13], (DEPTH, GLA_VAL, D_MODEL), GLA_VAL ** -0.5),
        "w_branch_b": nrm(ks[14], (DEPTH, SSM_INNER, D_MODEL), SSM_INNER ** -0.5),
        "w_out": nrm(ks[15], (DEPTH, D_MODEL, D_MODEL), D_MODEL ** -0.5),
        "ffn_norm": gain(ks[16], D_MODEL),
        "w_ffn_up": nrm(ks[17], (DEPTH, D_MODEL, 2 * FFN_HIDDEN), D_MODEL ** -0.5),
        "w_ffn_conv": nrm(ks[18], (DEPTH, FFN_CONV, FFN_HIDDEN), FFN_CONV ** -0.5),
        "b_ffn_conv": nrm(ks[19], (DEPTH, FFN_HIDDEN), 0.02),
        "w_ffn_down": nrm(ks[20], (DEPTH, FFN_HIDDEN, D_MODEL), FFN_HIDDEN ** -0.5),
        "ple_norm": gain(ks[21], D_MODEL),
        "w_ple_gate": nrm(ks[22], (DEPTH, D_MODEL, D_MODEL), D_MODEL ** -0.5),
        "w_ple_proj": nrm(ks[23], (DEPTH, PLE_DIM, D_MODEL), PLE_DIM ** -0.5),
        "final_norm": 1.0 + nrm(ks[24], (D_MODEL,), 0.02),
    }


def reference(x, p, mixer_norm, w_in, w_gla_gate, b_gla_gate, gla_norm, w_ssm_conv,
              b_ssm_conv, dt_bias, a_log, d_skip, ssm_norm, w_branch_a, w_branch_b,
              w_out, ffn_norm, w_ffn_up, w_ffn_conv, b_ffn_conv, w_ffn_down,
              ple_norm, w_ple_gate, w_ple_proj, final_norm):
    for i in range(DEPTH):
        h = rms_norm(x, mixer_norm[i])
        proj = h @ w_in[i]
        (q, k, v, g_out, a_lr, z, xbc, dt_raw, gate_a, gate_b) = split_cols(proj, IN_SPLITS)
        o_a = gla_mixer(q, k, v, g_out, a_lr, w_gla_gate[i], b_gla_gate[i], gla_norm[i])
        o_b = ssd_mixer(z, xbc, dt_raw, w_ssm_conv[i], b_ssm_conv[i], dt_bias[i],
                        a_log[i], d_skip[i], ssm_norm[i])
        merged = (jax.nn.sigmoid(gate_a) * (o_a @ w_branch_a[i])
                  + jax.nn.sigmoid(gate_b) * (o_b @ w_branch_b[i]))
        x = x + merged @ w_out[i]
        h = rms_norm(x, ffn_norm[i])
        act, lin = jnp.split(h @ w_ffn_up[i], 2, axis=-1)
        act = causal_dwconv(act, w_ffn_conv[i]) + b_ffn_conv[i]
        x = x + (jax.nn.gelu(act) * lin) @ w_ffn_down[i]
        g = jax.nn.sigmoid(rms_norm(x, ple_norm[i]) @ w_ple_gate[i])
        x = x + g * (p[i] @ w_ple_proj[i])
    return rms_norm(x, final_norm)
```

```python
import functools

import jax
import jax.numpy as jnp
from jax import lax
from jax.experimental import pallas as pl
from jax.experimental.pallas import tpu as pltpu

f32 = jnp.float32
bf16 = jnp.bfloat16

D_MODEL = 1024
GLA_HEADS = 4
GLA_DK = 128
GLA_DV = 256
GLA_KEY = GLA_HEADS * GLA_DK
GLA_VAL = GLA_HEADS * GLA_DV
GLA_GATE_RANK = 16
GLA_GATE_NORM = 16.0
GLA_CHUNK = 64
SSM_INNER = 2 * D_MODEL
SSM_HEADDIM = 64
SSM_HEADS = SSM_INNER // SSM_HEADDIM
SSM_GROUPS = 4
SSM_HPG = SSM_HEADS // SSM_GROUPS
SSM_STATE = 128
SSM_CONV = 4
SSM_CHUNK = 128
SSM_BC = SSM_GROUPS * SSM_STATE
SSM_XBC = SSM_INNER + 2 * SSM_BC
SSM_GROUP_WIDTH = SSM_INNER // SSM_GROUPS
FFN_HIDDEN = 2816
FFN_CONV = 3
PLE_DIM = 256
EPS = 1e-6

LANES = 128
SUBLANES = 8
VMEM_LIMIT = 56 * 1024 * 1024

OFF_Z = 0
OFF_Q = OFF_Z + SSM_INNER
OFF_K = OFF_Q + GLA_KEY
OFF_XBC = OFF_K + GLA_KEY
OFF_V = OFF_XBC + SSM_XBC
OFF_GO = OFF_V + GLA_VAL
OFF_GA = OFF_GO + GLA_VAL
OFF_GB = OFF_GA + D_MODEL
MAIN_WIDTH = OFF_GB + D_MODEL
assert OFF_XBC % SSM_XBC == 0 and OFF_V % GLA_DV == 0 and OFF_GA % D_MODEL == 0
SM_ALR = 0
SM_DT = 32

TM_PROJ = 1024
TN_PROJ = 1024
TB_GLA = 512
TB_SSD = 256
TM_MERGE = 512
TM_FFN = 512
FFN_CB = 256


def _split3(x):
    hi = x.astype(bf16)
    r1 = x - hi.astype(f32)
    mid = r1.astype(bf16)
    lo = (r1 - mid.astype(f32)).astype(bf16)
    return hi, mid, lo


def _dot(a, b):
    return jnp.dot(a, b, preferred_element_type=f32)


def _dot_exact_lhs(m, x):
    hi, mid, lo = _split3(x)
    return _dot(m, hi) + _dot(m, mid) + _dot(m, lo)


def _dot_exact_rhs(x, m):
    hi, mid, lo = _split3(x)
    return _dot(hi, m) + _dot(mid, m) + _dot(lo, m)


def _dot_nt(a, b):
    return lax.dot_general(a, b, (((1,), (1,)), ((), ())), preferred_element_type=f32)


def _dot_tn(a, b):
    return lax.dot_general(a, b, (((0,), (0,)), ((), ())), preferred_element_type=f32)


def _rms(x, gain):
    return x * lax.rsqrt(jnp.mean(x * x, axis=-1, keepdims=True) + EPS) * gain


def _silu(x):
    return x * jax.nn.sigmoid(x)


def _tri(n, dtype):
    r = lax.broadcasted_iota(jnp.int32, (n, n), 0)
    c = lax.broadcasted_iota(jnp.int32, (n, n), 1)
    return (r >= c).astype(dtype), r >= c


def _inproj_kernel(x_ref, g_ref, w_ref, ws_ref, o_ref, os_ref, h_ref):
    @pl.when(pl.program_id(1) == 0)
    def _():
        hb = _rms(x_ref[...], g_ref[...]).astype(bf16)
        h_ref[...] = hb
        os_ref[...] = _dot(hb, ws_ref[...])

    o_ref[...] = _dot(h_ref[...], w_ref[...]).astype(bf16)


def _inproj(x2, gain, w_main, w_small):
    n_tok = x2.shape[0]
    tm = min(TM_PROJ, n_tok)
    grid = (n_tok // tm, MAIN_WIDTH // TN_PROJ)
    return pl.pallas_call(
        _inproj_kernel,
        grid=grid,
        in_specs=[
            pl.BlockSpec((tm, D_MODEL), lambda i, j: (i, 0)),
            pl.BlockSpec((1, D_MODEL), lambda i, j: (0, 0)),
            pl.BlockSpec((D_MODEL, TN_PROJ), lambda i, j: (0, j)),
            pl.BlockSpec((D_MODEL, LANES), lambda i, j: (0, 0)),
        ],
        out_specs=[
            pl.BlockSpec((tm, TN_PROJ), lambda i, j: (i, j)),
            pl.BlockSpec((tm, LANES), lambda i, j: (i, 0)),
        ],
        out_shape=[
            jax.ShapeDtypeStruct((n_tok, MAIN_WIDTH), bf16),
            jax.ShapeDtypeStruct((n_tok, LANES), f32),
        ],
        scratch_shapes=[pltpu.VMEM((tm, D_MODEL), bf16)],
        compiler_params=pltpu.CompilerParams(
            dimension_semantics=("arbitrary", "arbitrary"),
            vmem_limit_bytes=VMEM_LIMIT),
        name="inproj",
    )(x2, gain, w_main, w_small)


def _gla_kernel(q_ref, k_ref, v_ref, go_ref, sm_ref, wg_ref, bg_ref, gn_ref,
                o_ref, st_ref, gl_ref):
    C = GLA_CHUNK

    @pl.when(pl.program_id(2) == 0)
    def _():
        st_ref[...] = jnp.zeros_like(st_ref)

    a_lr = sm_ref[:, SM_ALR:SM_ALR + GLA_GATE_RANK].astype(bf16)
    logit = _dot(a_lr, wg_ref[...].astype(bf16)) + bg_ref[...]
    gl_ref[...] = (jnp.minimum(logit, 0.0)
                   - jnp.log1p(jnp.exp(-jnp.abs(logit)))) * (1.0 / GLA_GATE_NORM)

    tri, causal = _tri(C, bf16)
    gain = gn_ref[...]
    n_chunks = q_ref.shape[0] // C

    def chunk(c, carry):
        rows = pl.ds(pl.multiple_of(c * C, C), C)
        b = _dot_exact_lhs(tri, gl_ref[rows, :])
        b_last = b[C - 1:C, :]
        q_t = (q_ref[rows, :].astype(f32) * (GLA_DK ** -0.5)) * jnp.exp(b)
        kf = k_ref[rows, :].astype(f32)
        k_t = kf * jnp.exp(-b)
        k_dec = kf * jnp.exp(b_last - b)
        v = v_ref[rows, :]
        q_tb = q_t.astype(bf16)
        att = jnp.where(causal, _dot_nt(q_tb, k_t.astype(bf16)), 0.0)
        st = st_ref[...]
        o = _dot(att.astype(bf16), v) + _dot_nt(q_tb, st.astype(bf16))
        st_ref[...] = jnp.exp(b_last) * st + _dot_tn(v, k_dec.astype(bf16))
        o = _rms(o, gain) * _silu(go_ref[rows, :].astype(f32))
        o_ref[rows, :] = o.astype(bf16)
        return carry

    lax.fori_loop(0, n_chunks, chunk, 0)


def _gla(proj, small, w_gate, b_gate, gla_norm, bsz, seq):
    n_tok = bsz * seq
    tb = min(TB_GLA, seq)
    n_t = seq // tb

    def row(b, h, t):
        return b * n_t + t

    return pl.pallas_call(
        _gla_kernel,
        grid=(bsz, GLA_HEADS, n_t),
        in_specs=[
            pl.BlockSpec((tb, GLA_DK), lambda b, h, t: (row(b, h, t), OFF_Q // GLA_DK + h)),
            pl.BlockSpec((tb, GLA_DK), lambda b, h, t: (row(b, h, t), OFF_K // GLA_DK + h)),
            pl.BlockSpec((tb, GLA_DV), lambda b, h, t: (row(b, h, t), OFF_V // GLA_DV + h)),
            pl.BlockSpec((tb, GLA_DV), lambda b, h, t: (row(b, h, t), OFF_GO // GLA_DV + h)),
            pl.BlockSpec((tb, LANES), lambda b, h, t: (row(b, h, t), 0)),
            pl.BlockSpec((GLA_GATE_RANK, GLA_DK), lambda b, h, t: (0, h)),
            pl.BlockSpec((1, GLA_DK), lambda b, h, t: (0, h)),
            pl.BlockSpec((1, GLA_DV), lambda b, h, t: (0, 0)),
        ],
        out_specs=pl.BlockSpec((tb, GLA_DV), lambda b, h, t: (row(b, h, t), h)),
        out_shape=jax.ShapeDtypeStruct((n_tok, GLA_VAL), bf16),
        scratch_shapes=[pltpu.VMEM((GLA_DV, GLA_DK), f32),
                        pltpu.VMEM((tb, GLA_DK), f32)],
        compiler_params=pltpu.CompilerParams(
            dimension_semantics=("arbitrary", "arbitrary", "arbitrary"),
            vmem_limit_bytes=VMEM_LIMIT),
        name="gla",
    )(proj, proj, proj, proj, small, w_gate, b_gate, gla_norm)


def _ssd_kernel(z_ref, xbc_ref, sm_ref, cw_ref, cb_ref, dtb_ref, arow_ref, dsk_ref,
                nrm_ref, e_ref, o_ref, st_ref, xe_ref, xc_ref):
    tb = xbc_ref.shape[0]
    L = SSM_CHUNK
    H = SUBLANES

    @pl.when(pl.program_id(1) == 0)
    def _():
        st_ref[...] = jnp.zeros_like(st_ref)
        xe_ref[0:H, :] = jnp.zeros((H, SSM_XBC), f32)

    CW = 512
    RW = 64
    xe_ref[H:H + tb, :] = xbc_ref[...].astype(f32)

    def conv_rows(r, carry):
        r0 = pl.multiple_of(r * RW, RW)
        for cb in range(SSM_XBC // CW):
            cols = slice(cb * CW, (cb + 1) * CW)
            blk = xe_ref[pl.ds(r0, RW + H), cols]
            acc = cb_ref[:, cols]
            for k in range(SSM_CONV):
                s = H - (SSM_CONV - 1) + k
                acc = acc + cw_ref[k:k + 1, cols] * blk[s:s + RW, :]
            xc_ref[pl.ds(r0, RW), cols] = _silu(acc)
        return carry

    lax.fori_loop(0, tb // RW, conv_rows, 0)
    xe_ref[0:H, :] = xe_ref[tb:tb + H, :]

    tri, causal = _tri(L, bf16)
    lane = lax.broadcasted_iota(jnp.int32, (L, LANES), 1)
    left = lane < SSM_HEADDIM

    def chunk(c, carry):
        rows = pl.ds(pl.multiple_of(c * L, L), L)
        dt = jax.nn.softplus(sm_ref[rows, :] + dtb_ref[...])
        a = dt * arow_ref[...]
        a_cs = _dot_exact_lhs(tri, a)
        a_last = a_cs[L - 1:L, :]
        a_cs_t = a_cs.T
        w1 = (dt * jnp.exp(a_last - a_cs)).astype(bf16)
        expa = jnp.exp(a_cs).astype(bf16)
        dtb = dt.astype(bf16)
        cd8 = jnp.broadcast_to(jnp.exp(a_last), (SUBLANES, LANES))

        for g in range(SSM_GROUPS):
            gc = slice(g * SSM_GROUP_WIDTH, (g + 1) * SSM_GROUP_WIDTH)
            e_g = e_ref[g]
            bm = xc_ref[rows, SSM_INNER + g * SSM_STATE:SSM_INNER + (g + 1) * SSM_STATE].astype(bf16)
            cm = xc_ref[rows, SSM_INNER + SSM_BC + g * SSM_STATE:
                        SSM_INNER + SSM_BC + (g + 1) * SSM_STATE].astype(bf16)
            xg = xc_ref[rows, gc]
            scores = _dot_nt(cm, bm)
            xdt = xg * _dot(dtb, e_g)
            xw = (xg * _dot(w1, e_g)).astype(bf16)
            expa_e = _dot(expa, e_g)
            cd_e = _dot_exact_rhs(cd8, e_g)[0:1, :]

            tiles = []
            for qd in range(SSM_HPG // 2):
                ms = []
                for j in (2 * qd, 2 * qd + 1):
                    ln = SM_DT + g * SSM_HPG + j
                    col = jnp.broadcast_to(a_cs[:, ln:ln + 1], (L, L))
                    rw = jnp.broadcast_to(a_cs_t[ln:ln + 1, :], (L, L))
                    ldec = jnp.exp(jnp.where(causal, col - rw, -jnp.inf))
                    ms.append((scores * ldec).astype(bf16))
                xt = xdt[:, qd * LANES:(qd + 1) * LANES]
                xcat = jnp.concatenate([jnp.where(left, xt, 0.0), jnp.where(left, 0.0, xt)],
                                       axis=0).astype(bf16)
                tiles.append(_dot(jnp.concatenate(ms, axis=1), xcat))
            y_diag = jnp.concatenate(tiles, axis=1)

            st = st_ref[g]
            y = y_diag + _dot(cm, st.astype(bf16)) * expa_e + xg * dsk_ref[:, gc]
            st_ref[g] = st * cd_e + _dot_tn(bm, xw)
            y = y * _silu(z_ref[rows, gc].astype(f32))
            y = _rms(y, nrm_ref[:, gc])
            o_ref[rows, gc] = y.astype(bf16)
        return carry

    lax.fori_loop(0, tb // L, chunk, 0)


def _ssd(proj, small, cw, cb, dtb_row, a_row, dsk_row, nrm, e_mat, bsz, seq):
    n_tok = bsz * seq
    tb = min(TB_SSD, seq)
    n_t = seq // tb

    def row(b, t):
        return b * n_t + t

    const2 = lambda b, t: (0, 0)
    return pl.pallas_call(
        _ssd_kernel,
        grid=(bsz, n_t),
        in_specs=[
            pl.BlockSpec((tb, SSM_INNER), lambda b, t: (row(b, t), OFF_Z // SSM_INNER)),
            pl.BlockSpec((tb, SSM_XBC), lambda b, t: (row(b, t), OFF_XBC // SSM_XBC)),
            pl.BlockSpec((tb, LANES), lambda b, t: (row(b, t), 0)),
            pl.BlockSpec((SSM_CONV, SSM_XBC), const2),
            pl.BlockSpec((1, SSM_XBC), const2),
            pl.BlockSpec((1, LANES), const2),
            pl.BlockSpec((1, LANES), const2),
            pl.BlockSpec((1, SSM_INNER), const2),
            pl.BlockSpec((1, SSM_INNER), const2),
            pl.BlockSpec((SSM_GROUPS, LANES, SSM_GROUP_WIDTH), lambda b, t: (0, 0, 0)),
        ],
        out_specs=pl.BlockSpec((tb, SSM_INNER), lambda b, t: (row(b, t), 0)),
        out_shape=jax.ShapeDtypeStruct((n_tok, SSM_INNER), bf16),
        scratch_shapes=[
            pltpu.VMEM((SSM_GROUPS, SSM_STATE, SSM_GROUP_WIDTH), f32),
            pltpu.VMEM((tb + SUBLANES, SSM_XBC), f32),
            pltpu.VMEM((tb, SSM_XBC), f32),
        ],
        compiler_params=pltpu.CompilerParams(
            dimension_semantics=("arbitrary", "arbitrary"),
            vmem_limit_bytes=VMEM_LIMIT),
        name="ssd",
    )(proj, proj, small, cw, cb, dtb_row, a_row, dsk_row, nrm, e_mat)


def _merge_kernel(oa_ref, ob_ref, ga_ref, gb_ref, x_ref, wa_ref, wb_ref, wo_ref, o_ref):
    ma = _dot(oa_ref[...], wa_ref[...])
    mb = _dot(ob_ref[...], wb_ref[...])
    merged = (jax.nn.sigmoid(ga_ref[...].astype(f32)) * ma
              + jax.nn.sigmoid(gb_ref[...].astype(f32)) * mb)
    o_ref[...] = x_ref[...] + _dot(merged.astype(bf16), wo_ref[...])


def _resident(shape):
    nd = len(shape)
    return pl.BlockSpec(shape, lambda *_: (0,) * nd, pipeline_mode=pl.Buffered(1))


def _merge(o_a, o_b, proj, x2, wa, wb, wo):
    n_tok = x2.shape[0]
    tm = min(TM_MERGE, n_tok)
    return pl.pallas_call(
        _merge_kernel,
        grid=(n_tok // tm,),
        in_specs=[
            pl.BlockSpec((tm, GLA_VAL), lambda i: (i, 0)),
            pl.BlockSpec((tm, SSM_INNER), lambda i: (i, 0)),
            pl.BlockSpec((tm, D_MODEL), lambda i: (i, OFF_GA // D_MODEL)),
            pl.BlockSpec((tm, D_MODEL), lambda i: (i, OFF_GB // D_MODEL)),
            pl.BlockSpec((tm, D_MODEL), lambda i: (i, 0)),
            _resident((GLA_VAL, D_MODEL)),
            _resident((SSM_INNER, D_MODEL)),
            _resident((D_MODEL, D_MODEL)),
        ],
        out_specs=pl.BlockSpec((tm, D_MODEL), lambda i: (i, 0)),
        out_shape=jax.ShapeDtypeStruct((n_tok, D_MODEL), f32),
        compiler_params=pltpu.CompilerParams(
            dimension_semantics=("arbitrary",),
            vmem_limit_bytes=VMEM_LIMIT),
        name="merge",
    )(o_a, o_b, proj, proj, x2, wa, wb, wo)


def _ffn_kernel(x_ref, p_ref, fn_ref, wup_ref, cw_ref, cb_ref, wdn_ref, pn_ref,
                wpg_ref, wpp_ref, fin_ref, o_ref, g_ref, halo_ref, ae_ref):
    tm = x_ref.shape[0]
    H = SUBLANES

    @pl.when(pl.program_id(1) == 0)
    def _():
        halo_ref[...] = jnp.zeros_like(halo_ref)

    x1 = x_ref[...]
    h = _rms(x1, fn_ref[...]).astype(bf16)
    for c in range(FFN_HIDDEN // FFN_CB):
        cols = slice(c * FFN_CB, (c + 1) * FFN_CB)
        act = _dot(h, wup_ref[:, cols])
        lin = _dot(h, wup_ref[:, FFN_HIDDEN + c * FFN_CB:FFN_HIDDEN + (c + 1) * FFN_CB])
        ae_ref[0:H, :] = halo_ref[:, cols]
        ae_ref[H:H + tm, :] = act
        halo_ref[:, cols] = act[tm - H:tm, :]
        conv = cb_ref[:, cols]
        for k in range(FFN_CONV):
            conv = conv + cw_ref[k:k + 1, cols] * ae_ref[pl.ds(H - (FFN_CONV - 1) + k, tm), :]
        g_ref[:, cols] = (jax.nn.gelu(conv) * lin).astype(bf16)

    x2 = x1 + _dot(g_ref[...], wdn_ref[...])
    gate = jax.nn.sigmoid(_dot(_rms(x2, pn_ref[...]).astype(bf16), wpg_ref[...]))
    x3 = x2 + gate * _dot(p_ref[...].astype(bf16), wpp_ref[...])
    o_ref[...] = _rms(x3, fin_ref[...])


def _ffn(x1, p2, ffn_norm, wup, cw, cb, wdn, ple_norm, wpg, wpp, final_norm, bsz, seq):
    n_tok = bsz * seq
    tm = min(TM_FFN, seq)
    n_t = seq // tm
    rowmap = lambda b, t: (b * n_t + t, 0)
    return pl.pallas_call(
        _ffn_kernel,
        grid=(bsz, n_t),
        in_specs=[
            pl.BlockSpec((tm, D_MODEL), rowmap),
            pl.BlockSpec((tm, PLE_DIM), rowmap),
            _resident((1, D_MODEL)),
            _resident((D_MODEL, 2 * FFN_HIDDEN)),
            _resident((FFN_CONV, FFN_HIDDEN)),
            _resident((1, FFN_HIDDEN)),
            _resident((FFN_HIDDEN, D_MODEL)),
            _resident((1, D_MODEL)),
            _resident((D_MODEL, D_MODEL)),
            _resident((PLE_DIM, D_MODEL)),
            _resident((1, D_MODEL)),
        ],
        out_specs=pl.BlockSpec((tm, D_MODEL), rowmap),
        out_shape=jax.ShapeDtypeStruct((n_tok, D_MODEL), f32),
        scratch_shapes=[
            pltpu.VMEM((tm, FFN_HIDDEN), bf16),
            pltpu.VMEM((SUBLANES, FFN_HIDDEN), f32),
            pltpu.VMEM((tm + SUBLANES, FFN_CB), f32),
        ],
        compiler_params=pltpu.CompilerParams(
            dimension_semantics=("arbitrary", "arbitrary"),
            vmem_limit_bytes=VMEM_LIMIT),
        name="ffn",
    )(x1, p2, ffn_norm, wup, cw, cb, wdn, ple_norm, wpg, wpp, final_norm)


def _lane_row(vals, offset):
    return jnp.zeros((1, LANES), f32).at[0, offset:offset + vals.shape[0]].set(vals.astype(f32))


def _expand_mats():
    src = jnp.arange(LANES)[None, :, None]
    dst = jnp.arange(SSM_GROUP_WIDTH)[None, None, :]
    g = jnp.arange(SSM_GROUPS)[:, None, None]
    return (src == SM_DT + g * SSM_HPG + dst // SSM_HEADDIM).astype(bf16)


def _layer(x2, p2, bsz, seq, mixer_norm, w_in, w_gla_gate, b_gla_gate, gla_norm, w_ssm_conv,
           b_ssm_conv, dt_bias, a_log, d_skip, ssm_norm, w_branch_a, w_branch_b, w_out,
           ffn_norm, w_ffn_up, w_ffn_conv, b_ffn_conv, w_ffn_down, ple_norm, w_ple_gate,
           w_ple_proj, out_norm):
    sizes = (GLA_KEY, GLA_KEY, GLA_VAL, GLA_VAL, GLA_GATE_RANK, SSM_INNER, SSM_XBC, SSM_HEADS,
             D_MODEL, D_MODEL)
    offs = [0]
    for s in sizes:
        offs.append(offs[-1] + s)
    part = [w_in[:, offs[i]:offs[i + 1]] for i in range(len(sizes))]
    w_main = jnp.concatenate([part[5], part[0], part[1], part[6], part[2], part[3], part[8], part[9]],
                             axis=1).astype(bf16)
    w_small = jnp.zeros((D_MODEL, LANES), f32)
    w_small = w_small.at[:, SM_ALR:SM_ALR + GLA_GATE_RANK].set(part[4])
    w_small = w_small.at[:, SM_DT:SM_DT + SSM_HEADS].set(part[7]).astype(bf16)

    proj, small = _inproj(x2, mixer_norm[None, :], w_main, w_small)

    o_a = _gla(proj, small, w_gla_gate, b_gla_gate[None, :], gla_norm[None, :], bsz, seq)

    o_b = _ssd(proj, small, w_ssm_conv, b_ssm_conv[None, :],
               _lane_row(dt_bias, SM_DT), _lane_row(-jnp.exp(a_log.astype(f32)), SM_DT),
               jnp.repeat(d_skip.astype(f32), SSM_HEADDIM)[None, :], ssm_norm[None, :],
               _expand_mats(), bsz, seq)

    x1 = _merge(o_a, o_b, proj, x2, w_branch_a.astype(bf16), w_branch_b.astype(bf16),
                w_out.astype(bf16))

    return _ffn(x1, p2, ffn_norm[None, :], w_ffn_up.astype(bf16), w_ffn_conv, b_ffn_conv[None, :],
                w_ffn_down.astype(bf16), ple_norm[None, :], w_ple_gate.astype(bf16),
                w_ple_proj.astype(bf16), out_norm[None, :], bsz, seq)


def kernel(x, p, mixer_norm, w_in, w_gla_gate, b_gla_gate, gla_norm, w_ssm_conv, b_ssm_conv,
           dt_bias, a_log, d_skip, ssm_norm, w_branch_a, w_branch_b, w_out, ffn_norm, w_ffn_up,
           w_ffn_conv, b_ffn_conv, w_ffn_down, ple_norm, w_ple_gate, w_ple_proj, final_norm):
    bsz, seq, _ = x.shape
    depth = w_in.shape[0]
    assert depth == 1, "the fused final RMSNorm assumes a single layer"
    x2 = x.reshape(bsz * seq, D_MODEL)
    out = _layer(x2, p[0].reshape(bsz * seq, PLE_DIM), bsz, seq, mixer_norm[0], w_in[0],
                 w_gla_gate[0], b_gla_gate[0], gla_norm[0], w_ssm_conv[0], b_ssm_conv[0],
                 dt_bias[0], a_log[0], d_skip[0], ssm_norm[0], w_branch_a[0], w_branch_b[0],
                 w_out[0], ffn_norm[0], w_ffn_up[0], w_ffn_conv[0], b_ffn_conv[0], w_ffn_down[0],
                 ple_norm[0], w_ple_gate[0], w_ple_proj[0], final_norm)
    return out.reshape(bsz, seq, D_MODEL)
```

```python
import functools

import jax
import jax.numpy as jnp
from jax import lax
from jax.experimental import pallas as pl
from jax.experimental.pallas import tpu as pltpu

f32 = jnp.float32
bf16 = jnp.bfloat16

D_MODEL = 1024
GLA_HEADS = 4
GLA_DK = 128
GLA_DV = 256
GLA_KEY = GLA_HEADS * GLA_DK
GLA_VAL = GLA_HEADS * GLA_DV
GLA_GATE_RANK = 16
GLA_GATE_NORM = 16.0
GLA_CHUNK = 64
SSM_INNER = 2 * D_MODEL
SSM_HEADDIM = 64
SSM_HEADS = SSM_INNER // SSM_HEADDIM
SSM_GROUPS = 4
SSM_HPG = SSM_HEADS // SSM_GROUPS
SSM_STATE = 128
SSM_CONV = 4
SSM_CHUNK = 128
SSM_BC = SSM_GROUPS * SSM_STATE
SSM_XBC = SSM_INNER + 2 * SSM_BC
SSM_GROUP_WIDTH = SSM_INNER // SSM_GROUPS
FFN_HIDDEN = 2816
FFN_CONV = 3
PLE_DIM = 256
EPS = 1e-6

LANES = 128
SUBLANES = 8
VMEM_LIMIT = 56 * 1024 * 1024

OFF_Z = 0
OFF_Q = OFF_Z + SSM_INNER
OFF_K = OFF_Q + GLA_KEY
OFF_XBC = OFF_K + GLA_KEY
OFF_V = OFF_XBC + SSM_XBC
OFF_GO = OFF_V + GLA_VAL
OFF_GA = OFF_GO + GLA_VAL
OFF_GB = OFF_GA + D_MODEL
MAIN_WIDTH = OFF_GB + D_MODEL
assert OFF_XBC % SSM_XBC == 0 and OFF_V % GLA_DV == 0 and OFF_GA % D_MODEL == 0
SM_ALR = 0
SM_DT = 32

TM_PROJ = 1024
TN_PROJ = 1024
TB_GLA = 512
TB_SSD = 256
TM_MERGE = 512
TM_FFN = 512
FFN_CB = 256


def _split3(x):
    hi = x.astype(bf16)
    r1 = x - hi.astype(f32)
    mid = r1.astype(bf16)
    lo = (r1 - mid.astype(f32)).astype(bf16)
    return hi, mid, lo


def _dot(a, b):
    return jnp.dot(a, b, preferred_element_type=f32)


def _dot_exact_lhs(m, x):
    hi, mid, lo = _split3(x)
    return _dot(m, hi) + _dot(m, mid) + _dot(m, lo)


def _dot_exact_rhs(x, m):
    hi, mid, lo = _split3(x)
    return _dot(hi, m) + _dot(mid, m) + _dot(lo, m)


def _dot_nt(a, b):
    return lax.dot_general(a, b, (((1,), (1,)), ((), ())), preferred_element_type=f32)


def _dot_tn(a, b):
    return lax.dot_general(a, b, (((0,), (0,)), ((), ())), preferred_element_type=f32)


def _rms(x, gain):
    return x * lax.rsqrt(jnp.mean(x * x, axis=-1, keepdims=True) + EPS) * gain


def _silu(x):
    h = 0.5 * x
    return h + h * jnp.tanh(h)


def _tri(n, dtype):
    r = lax.broadcasted_iota(jnp.int32, (n, n), 0)
    c = lax.broadcasted_iota(jnp.int32, (n, n), 1)
    return (r >= c).astype(dtype), r >= c


PROJ_HALO = 16
PROJ_CS = 256
PROJ_RC = 128
_SILU_BLOCKS = tuple(range(OFF_Z // TN_PROJ, (OFF_Z + SSM_INNER) // TN_PROJ)) + \
    tuple(range(OFF_GO // TN_PROJ, (OFF_GO + GLA_VAL) // TN_PROJ))
_CONV_BLOCK0 = OFF_XBC // TN_PROJ
_CONV_BLOCKS = SSM_XBC // TN_PROJ
assert OFF_Z % TN_PROJ == 0 and OFF_GO % TN_PROJ == 0 and OFF_XBC % TN_PROJ == 0
assert SSM_XBC % TN_PROJ == 0 and SSM_INNER % TN_PROJ == 0 and GLA_VAL % TN_PROJ == 0


def _inproj_kernel(x_ref, xp_ref, g_ref, w_ref, ws_ref, cw_ref, cb_ref, o_ref, os_ref, h_ref,
                   r_ref, *, tiles_per_seq):
    i = pl.program_id(0)
    j = pl.program_id(1)
    tm = x_ref.shape[0]

    @pl.when(j == 0)
    def _():
        hb = _rms(x_ref[...], g_ref[...]).astype(bf16)
        h_ref[PROJ_HALO:, :] = hb
        os_ref[...] = _dot(hb, ws_ref[...])
        hp = _rms(xp_ref[...], g_ref[...])
        first = (i % tiles_per_seq) == 0
        h_ref[0:PROJ_HALO, :] = jnp.where(first, 0.0, hp).astype(bf16)

    is_conv = (j >= _CONV_BLOCK0) & (j < _CONV_BLOCK0 + _CONV_BLOCKS)
    is_silu = functools.reduce(lambda a, b: a | b, [j == b for b in _SILU_BLOCKS])

    @pl.when(is_conv)
    def _():
        n_cs = TN_PROJ // PROJ_CS
        col = [slice(cs * PROJ_CS, (cs + 1) * PROJ_CS) for cs in range(n_cs)]

        def conv(cs):
            cols = col[cs]
            for rc in range(tm // PROJ_RC):
                sub = r_ref[cs % 2, rc * PROJ_RC:rc * PROJ_RC + PROJ_HALO + PROJ_RC, :]
                acc = cb_ref[:, cols] + cw_ref[SSM_CONV - 1:SSM_CONV, cols] * sub[PROJ_HALO:, :]
                for k in range(SSM_CONV - 1):
                    shifted = pltpu.roll(sub, SSM_CONV - 1 - k, 0)[PROJ_HALO:, :]
                    acc = acc + cw_ref[k:k + 1, cols] * shifted
                o_ref[rc * PROJ_RC:(rc + 1) * PROJ_RC, cols] = _silu(acc).astype(bf16)

        r_ref[0] = _dot(h_ref[...], w_ref[:, col[0]])
        for cs in range(n_cs):
            if cs + 1 < n_cs:
                r_ref[(cs + 1) % 2] = _dot(h_ref[...], w_ref[:, col[cs + 1]])
            conv(cs)

    @pl.when(is_silu)
    def _():
        o_ref[...] = _silu(_dot(h_ref[PROJ_HALO:, :], w_ref[...])).astype(bf16)

    @pl.when(jnp.logical_not(is_conv | is_silu))
    def _():
        o_ref[...] = _dot(h_ref[PROJ_HALO:, :], w_ref[...]).astype(bf16)


def _inproj(x2, gain, w_main, w_small, cw, cb, seq):
    n_tok = x2.shape[0]
    tm = min(TM_PROJ, seq)
    grid = (n_tok // tm, MAIN_WIDTH // TN_PROJ)
    halo_blocks = tm // PROJ_HALO

    def conv_block(i, j):
        return (0, jnp.clip(j - _CONV_BLOCK0, 0, _CONV_BLOCKS - 1))

    return pl.pallas_call(
        functools.partial(_inproj_kernel, tiles_per_seq=seq // tm),
        grid=grid,
        in_specs=[
            pl.BlockSpec((tm, D_MODEL), lambda i, j: (i, 0)),
            pl.BlockSpec((PROJ_HALO, D_MODEL), lambda i, j: (jnp.maximum(i * halo_blocks - 1, 0), 0)),
            pl.BlockSpec((1, D_MODEL), lambda i, j: (0, 0)),
            pl.BlockSpec((D_MODEL, TN_PROJ), lambda i, j: (0, j)),
            pl.BlockSpec((D_MODEL, LANES), lambda i, j: (0, 0)),
            pl.BlockSpec((SSM_CONV, TN_PROJ), conv_block),
            pl.BlockSpec((1, TN_PROJ), conv_block),
        ],
        out_specs=[
            pl.BlockSpec((tm, TN_PROJ), lambda i, j: (i, j)),
            pl.BlockSpec((tm, LANES), lambda i, j: (i, 0)),
        ],
        out_shape=[
            jax.ShapeDtypeStruct((n_tok, MAIN_WIDTH), bf16),
            jax.ShapeDtypeStruct((n_tok, LANES), f32),
        ],
        scratch_shapes=[pltpu.VMEM((PROJ_HALO + tm, D_MODEL), bf16),
                        pltpu.VMEM((2, PROJ_HALO + tm, PROJ_CS), f32)],
        compiler_params=pltpu.CompilerParams(
            dimension_semantics=("arbitrary", "arbitrary"),
            vmem_limit_bytes=VMEM_LIMIT),
        name="inproj",
    )(x2, x2, gain, w_main, w_small, cw, cb)


def _gla_kernel(q_ref, k_ref, v_ref, go_ref, sm_ref, wg_ref, bg_ref, gn_ref,
                o_ref, st_ref, gl_ref):
    C = GLA_CHUNK

    @pl.when(pl.program_id(2) == 0)
    def _():
        st_ref[...] = jnp.zeros_like(st_ref)

    a_lr = sm_ref[:, SM_ALR:SM_ALR + GLA_GATE_RANK].astype(bf16)
    logit = _dot(a_lr, wg_ref[...].astype(bf16)) + bg_ref[...]
    gl_ref[...] = (jnp.minimum(logit, 0.0)
                   - jnp.log1p(jnp.exp(-jnp.abs(logit)))) * (1.0 / GLA_GATE_NORM)

    tri, causal = _tri(C, bf16)
    gain = gn_ref[...]
    n_chunks = q_ref.shape[0] // C

    chunks = [slice(c * C, (c + 1) * C) for c in range(n_chunks)]
    b3s = []
    for rows in chunks:
        hi, mid, lo = _split3(gl_ref[rows, :])
        b3s.append(_dot(tri, jnp.concatenate([hi, mid, lo], axis=1)))
    q_tb, k_tb, k_db, decay = [], [], [], []
    for rows, b3 in zip(chunks, b3s):
        b = (b3[:, 0:GLA_DK] + b3[:, GLA_DK:2 * GLA_DK]) + b3[:, 2 * GLA_DK:3 * GLA_DK]
        b_last = b[C - 1:C, :]
        kf = k_ref[rows, :].astype(f32)
        q_tb.append(((q_ref[rows, :].astype(f32) * (GLA_DK ** -0.5)) * jnp.exp(b)).astype(bf16))
        k_tb.append((kf * jnp.exp(-b)).astype(bf16))
        k_db.append((kf * jnp.exp(b_last - b)).astype(bf16))
        decay.append(jnp.exp(b_last))
    att = [_dot_nt(q_tb[c], k_tb[c]) for c in range(n_chunks)]
    upd = [_dot_tn(v_ref[chunks[c], :], k_db[c]) for c in range(n_chunks)]
    o_intra = [_dot(jnp.where(causal, att[c], 0.0).astype(bf16), v_ref[chunks[c], :])
               for c in range(n_chunks)]
    st = st_ref[...]
    for c, rows in enumerate(chunks):
        o = o_intra[c] + _dot_nt(q_tb[c], st.astype(bf16))
        st = decay[c] * st + upd[c]
        o = _rms(o, gain) * go_ref[rows, :].astype(f32)
        o_ref[rows, :] = o.astype(bf16)
    st_ref[...] = st


def _gla(proj, small, w_gate, b_gate, gla_norm, bsz, seq):
    n_tok = bsz * seq
    tb = min(TB_GLA, seq)
    n_t = seq // tb

    def row(b, h, t):
        return b * n_t + t

    return pl.pallas_call(
        _gla_kernel,
        grid=(bsz, GLA_HEADS, n_t),
        in_specs=[
            pl.BlockSpec((tb, GLA_DK), lambda b, h, t: (row(b, h, t), OFF_Q // GLA_DK + h)),
            pl.BlockSpec((tb, GLA_DK), lambda b, h, t: (row(b, h, t), OFF_K // GLA_DK + h)),
            pl.BlockSpec((tb, GLA_DV), lambda b, h, t: (row(b, h, t), OFF_V // GLA_DV + h)),
            pl.BlockSpec((tb, GLA_DV), lambda b, h, t: (row(b, h, t), OFF_GO // GLA_DV + h)),
            pl.BlockSpec((tb, LANES), lambda b, h, t: (row(b, h, t), 0)),
            pl.BlockSpec((GLA_GATE_RANK, GLA_DK), lambda b, h, t: (0, h)),
            pl.BlockSpec((1, GLA_DK), lambda b, h, t: (0, h)),
            pl.BlockSpec((1, GLA_DV), lambda b, h, t: (0, 0)),
        ],
        out_specs=pl.BlockSpec((tb, GLA_DV), lambda b, h, t: (row(b, h, t), h)),
        out_shape=jax.ShapeDtypeStruct((n_tok, GLA_VAL), bf16),
        scratch_shapes=[pltpu.VMEM((GLA_DV, GLA_DK), f32),
                        pltpu.VMEM((tb, GLA_DK), f32)],
        compiler_params=pltpu.CompilerParams(
            dimension_semantics=("arbitrary", "arbitrary", "arbitrary"),
            vmem_limit_bytes=VMEM_LIMIT),
        name="gla",
    )(proj, proj, proj, proj, small, w_gate, b_gate, gla_norm)


def _ssd_kernel(z_ref, xc_ref, sm_ref, dtb_ref, arow_ref, dsk_ref, nrm_ref, e_ref,
                o_ref, st_ref):
    tb = xc_ref.shape[0]
    L = SSM_CHUNK

    @pl.when(pl.program_id(1) == 0)
    def _():
        st_ref[...] = jnp.zeros_like(st_ref)

    tri, causal = _tri(L, bf16)
    lane = lax.broadcasted_iota(jnp.int32, (L, LANES), 1)
    left = lane < SSM_HEADDIM

    n_chunks = tb // L
    chunks = [slice(c * L, (c + 1) * L) for c in range(n_chunks)]
    groups = [slice(g * SSM_GROUP_WIDTH, (g + 1) * SSM_GROUP_WIDTH) for g in range(SSM_GROUPS)]
    CD = 2 * SUBLANES

    dts = [jax.nn.softplus(sm_ref[rows, :] + dtb_ref[...]) for rows in chunks]
    parts = []
    for dt in dts:
        parts.extend(_split3(dt * arow_ref[...]))
    cs_all = _dot(tri, jnp.concatenate(parts, axis=1))
    a_cs, a_cs_t, lhs_main, lhs_cd = [], [], [], []
    for c, dt in enumerate(dts):
        o3 = 3 * LANES * c
        acs = (cs_all[:, o3:o3 + LANES] + cs_all[:, o3 + LANES:o3 + 2 * LANES]) \
            + cs_all[:, o3 + 2 * LANES:o3 + 3 * LANES]
        a_last = acs[L - 1:L, :]
        a_cs.append(acs)
        a_cs_t.append(acs.T)
        lhs_main += [dt.astype(bf16), (dt * jnp.exp(a_last - acs)).astype(bf16),
                     jnp.exp(acs).astype(bf16)]
        lhs_cd += list(_split3(jnp.broadcast_to(jnp.exp(a_last), (CD, LANES))))
    lhs = jnp.concatenate(lhs_main + lhs_cd, axis=0)

    exp_g = [_dot(lhs, e_ref[g]) for g in range(SSM_GROUPS)]

    def expanded(g, c, k):
        r0 = (3 * c + k) * L
        return exp_g[g][r0:r0 + L, :]

    def chunk_decay(g, c):
        r0 = 3 * L * n_chunks + 3 * CD * c
        r = exp_g[g]
        return (r[r0:r0 + 1, :] + r[r0 + CD:r0 + CD + 1, :]) + r[r0 + 2 * CD:r0 + 2 * CD + 1, :]

    pairs = [(c, g) for c in range(n_chunks) for g in range(SSM_GROUPS)]
    bm, cm, xg, scores, xdt, upd = {}, {}, {}, {}, {}, {}
    for c, g in pairs:
        rows = chunks[c]
        bm[c, g] = xc_ref[rows, SSM_INNER + g * SSM_STATE:SSM_INNER + (g + 1) * SSM_STATE]
        cm[c, g] = xc_ref[rows, SSM_INNER + SSM_BC + g * SSM_STATE:
                          SSM_INNER + SSM_BC + (g + 1) * SSM_STATE]
        xg[c, g] = xc_ref[rows, groups[g]].astype(f32)
        scores[c, g] = _dot_nt(cm[c, g], bm[c, g])
        xdt[c, g] = xg[c, g] * expanded(g, c, 0)
        upd[c, g] = _dot_tn(bm[c, g], (xg[c, g] * expanded(g, c, 1)).astype(bf16))

    y_diag = {}
    for c, g in pairs:
        tiles = []
        for qd in range(SSM_HPG // 2):
            ms = []
            for j in (2 * qd, 2 * qd + 1):
                ln = SM_DT + g * SSM_HPG + j
                col = jnp.broadcast_to(a_cs[c][:, ln:ln + 1], (L, L))
                rw = jnp.broadcast_to(a_cs_t[c][ln:ln + 1, :], (L, L))
                ldec = jnp.exp(jnp.where(causal, col - rw, -jnp.inf))
                ms.append((scores[c, g] * ldec).astype(bf16))
            xt = xdt[c, g][:, qd * LANES:(qd + 1) * LANES]
            xcat = jnp.concatenate([jnp.where(left, xt, 0.0), jnp.where(left, 0.0, xt)],
                                   axis=0).astype(bf16)
            tiles.append(_dot(jnp.concatenate(ms, axis=1), xcat))
        y_diag[c, g] = jnp.concatenate(tiles, axis=1)

    for g in range(SSM_GROUPS):
        st = st_ref[g]
        for c in range(n_chunks):
            rows = chunks[c]
            y = (y_diag[c, g] + _dot(cm[c, g], st.astype(bf16)) * expanded(g, c, 2)
                 + xg[c, g] * dsk_ref[:, groups[g]])
            st = st * chunk_decay(g, c) + upd[c, g]
            y = y * z_ref[rows, groups[g]].astype(f32)
            y = _rms(y, nrm_ref[:, groups[g]])
            o_ref[rows, groups[g]] = y.astype(bf16)
        st_ref[g] = st


def _ssd(proj, small, dtb_row, a_row, dsk_row, nrm, e_mat, bsz, seq):
    n_tok = bsz * seq
    tb = min(TB_SSD, seq)
    n_t = seq // tb

    def row(b, t):
        return b * n_t + t

    const2 = lambda b, t: (0, 0)
    return pl.pallas_call(
        _ssd_kernel,
        grid=(bsz, n_t),
        in_specs=[
            pl.BlockSpec((tb, SSM_INNER), lambda b, t: (row(b, t), OFF_Z // SSM_INNER)),
            pl.BlockSpec((tb, SSM_XBC), lambda b, t: (row(b, t), OFF_XBC // SSM_XBC)),
            pl.BlockSpec((tb, LANES), lambda b, t: (row(b, t), 0)),
            pl.BlockSpec((1, LANES), const2),
            pl.BlockSpec((1, LANES), const2),
            pl.BlockSpec((1, SSM_INNER), const2),
            pl.BlockSpec((1, SSM_INNER), const2),
            pl.BlockSpec((SSM_GROUPS, LANES, SSM_GROUP_WIDTH), lambda b, t: (0, 0, 0)),
        ],
        out_specs=pl.BlockSpec((tb, SSM_INNER), lambda b, t: (row(b, t), 0)),
        out_shape=jax.ShapeDtypeStruct((n_tok, SSM_INNER), bf16),
        scratch_shapes=[pltpu.VMEM((SSM_GROUPS, SSM_STATE, SSM_GROUP_WIDTH), f32)],
        compiler_params=pltpu.CompilerParams(
            dimension_semantics=("arbitrary", "arbitrary"),
            vmem_limit_bytes=VMEM_LIMIT),
        name="ssd",
    )(proj, proj, small, dtb_row, a_row, dsk_row, nrm, e_mat)


def _merge_kernel(oa_ref, ob_ref, ga_ref, gb_ref, x_ref, wa_ref, wb_ref, wo_ref, o_ref):
    ma = _dot(oa_ref[...], wa_ref[...])
    mb = _dot(ob_ref[...], wb_ref[...])
    merged = (jax.nn.sigmoid(ga_ref[...].astype(f32)) * ma
              + jax.nn.sigmoid(gb_ref[...].astype(f32)) * mb)
    o_ref[...] = x_ref[...] + _dot(merged.astype(bf16), wo_ref[...])


def _resident(shape):
    nd = len(shape)
    return pl.BlockSpec(shape, lambda *_: (0,) * nd, pipeline_mode=pl.Buffered(1))


def _merge(o_a, o_b, proj, x2, wa, wb, wo):
    n_tok = x2.shape[0]
    tm = min(TM_MERGE, n_tok)
    return pl.pallas_call(
        _merge_kernel,
        grid=(n_tok // tm,),
        in_specs=[
            pl.BlockSpec((tm, GLA_VAL), lambda i: (i, 0)),
            pl.BlockSpec((tm, SSM_INNER), lambda i: (i, 0)),
            pl.BlockSpec((tm, D_MODEL), lambda i: (i, OFF_GA // D_MODEL)),
            pl.BlockSpec((tm, D_MODEL), lambda i: (i, OFF_GB // D_MODEL)),
            pl.BlockSpec((tm, D_MODEL), lambda i: (i, 0)),
            _resident((GLA_VAL, D_MODEL)),
            _resident((SSM_INNER, D_MODEL)),
            _resident((D_MODEL, D_MODEL)),
        ],
        out_specs=pl.BlockSpec((tm, D_MODEL), lambda i: (i, 0)),
        out_shape=jax.ShapeDtypeStruct((n_tok, D_MODEL), f32),
        compiler_params=pltpu.CompilerParams(
            dimension_semantics=("arbitrary",),
            vmem_limit_bytes=VMEM_LIMIT),
        name="merge",
    )(o_a, o_b, proj, proj, x2, wa, wb, wo)


def _ffn_kernel(x_ref, p_ref, fn_ref, wup_ref, cw_ref, cb_ref, wdn_ref, pn_ref,
                wpg_ref, wpp_ref, fin_ref, o_ref, g_ref, halo_ref, ae_ref):
    tm = x_ref.shape[0]
    H = SUBLANES

    @pl.when(pl.program_id(1) == 0)
    def _():
        halo_ref[...] = jnp.zeros_like(halo_ref)

    x1 = x_ref[...]
    h = _rms(x1, fn_ref[...]).astype(bf16)
    for c in range(FFN_HIDDEN // FFN_CB):
        cols = slice(c * FFN_CB, (c + 1) * FFN_CB)
        act = _dot(h, wup_ref[:, cols])
        lin = _dot(h, wup_ref[:, FFN_HIDDEN + c * FFN_CB:FFN_HIDDEN + (c + 1) * FFN_CB])
        ae_ref[0:H, :] = halo_ref[:, cols]
        ae_ref[H:H + tm, :] = act
        halo_ref[:, cols] = act[tm - H:tm, :]
        conv = cb_ref[:, cols]
        for k in range(FFN_CONV):
            conv = conv + cw_ref[k:k + 1, cols] * ae_ref[pl.ds(H - (FFN_CONV - 1) + k, tm), :]
        g_ref[:, cols] = (jax.nn.gelu(conv) * lin).astype(bf16)

    x2 = x1 + _dot(g_ref[...], wdn_ref[...])
    gate = jax.nn.sigmoid(_dot(_rms(x2, pn_ref[...]).astype(bf16), wpg_ref[...]))
    x3 = x2 + gate * _dot(p_ref[...].astype(bf16), wpp_ref[...])
    o_ref[...] = _rms(x3, fin_ref[...])


def _ffn(x1, p2, ffn_norm, wup, cw, cb, wdn, ple_norm, wpg, wpp, final_norm, bsz, seq):
    n_tok = bsz * seq
    tm = min(TM_FFN, seq)
    n_t = seq // tm
    rowmap = lambda b, t: (b * n_t + t, 0)
    return pl.pallas_call(
        _ffn_kernel,
        grid=(bsz, n_t),
        in_specs=[
            pl.BlockSpec((tm, D_MODEL), rowmap),
            pl.BlockSpec((tm, PLE_DIM), rowmap),
            _resident((1, D_MODEL)),
            _resident((D_MODEL, 2 * FFN_HIDDEN)),
            _resident((FFN_CONV, FFN_HIDDEN)),
            _resident((1, FFN_HIDDEN)),
            _resident((FFN_HIDDEN, D_MODEL)),
            _resident((1, D_MODEL)),
            _resident((D_MODEL, D_MODEL)),
            _resident((PLE_DIM, D_MODEL)),
            _resident((1, D_MODEL)),
        ],
        out_specs=pl.BlockSpec((tm, D_MODEL), rowmap),
        out_shape=jax.ShapeDtypeStruct((n_tok, D_MODEL), f32),
        scratch_shapes=[
            pltpu.VMEM((tm, FFN_HIDDEN), bf16),
            pltpu.VMEM((SUBLANES, FFN_HIDDEN), f32),
            pltpu.VMEM((tm + SUBLANES, FFN_CB), f32),
        ],
        compiler_params=pltpu.CompilerParams(
            dimension_semantics=("arbitrary", "arbitrary"),
            vmem_limit_bytes=VMEM_LIMIT),
        name="ffn",
    )(x1, p2, ffn_norm, wup, cw, cb, wdn, ple_norm, wpg, wpp, final_norm)


def _lane_row(vals, offset):
    return jnp.zeros((1, LANES), f32).at[0, offset:offset + vals.shape[0]].set(vals.astype(f32))


def _expand_mats():
    src = jnp.arange(LANES)[None, :, None]
    dst = jnp.arange(SSM_GROUP_WIDTH)[None, None, :]
    g = jnp.arange(SSM_GROUPS)[:, None, None]
    return (src == SM_DT + g * SSM_HPG + dst // SSM_HEADDIM).astype(bf16)


def _layer(x2, p2, bsz, seq, mixer_norm, w_in, w_gla_gate, b_gla_gate, gla_norm, w_ssm_conv,
           b_ssm_conv, dt_bias, a_log, d_skip, ssm_norm, w_branch_a, w_branch_b, w_out,
           ffn_norm, w_ffn_up, w_ffn_conv, b_ffn_conv, w_ffn_down, ple_norm, w_ple_gate,
           w_ple_proj, out_norm):
    sizes = (GLA_KEY, GLA_KEY, GLA_VAL, GLA_VAL, GLA_GATE_RANK, SSM_INNER, SSM_XBC, SSM_HEADS,
             D_MODEL, D_MODEL)
    offs = [0]
    for s in sizes:
        offs.append(offs[-1] + s)
    part = [w_in[:, offs[i]:offs[i + 1]] for i in range(len(sizes))]
    w_main = jnp.concatenate([part[5], part[0], part[1], part[6], part[2], part[3], part[8], part[9]],
                             axis=1).astype(bf16)
    w_small = jnp.zeros((D_MODEL, LANES), f32)
    w_small = w_small.at[:, SM_ALR:SM_ALR + GLA_GATE_RANK].set(part[4])
    w_small = w_small.at[:, SM_DT:SM_DT + SSM_HEADS].set(part[7]).astype(bf16)

    proj, small = _inproj(x2, mixer_norm[None, :], w_main, w_small, w_ssm_conv,
                          b_ssm_conv[None, :], seq)

    o_a = _gla(proj, small, w_gla_gate, b_gla_gate[None, :], gla_norm[None, :], bsz, seq)

    o_b = _ssd(proj, small,
               _lane_row(dt_bias, SM_DT), _lane_row(-jnp.exp(a_log.astype(f32)), SM_DT),
               jnp.repeat(d_skip.astype(f32), SSM_HEADDIM)[None, :], ssm_norm[None, :],
               _expand_mats(), bsz, seq)

    x1 = _merge(o_a, o_b, proj, x2, w_branch_a.astype(bf16), w_branch_b.astype(bf16),
                w_out.astype(bf16))

    return _ffn(x1, p2, ffn_norm[None, :], w_ffn_up.astype(bf16), w_ffn_conv, b_ffn_conv[None, :],
                w_ffn_down.astype(bf16), ple_norm[None, :], w_ple_gate.astype(bf16),
                w_ple_proj.astype(bf16), out_norm[None, :], bsz, seq)


def kernel(x, p, mixer_norm, w_in, w_gla_gate, b_gla_gate, gla_norm, w_ssm_conv, b_ssm_conv,
           dt_bias, a_log, d_skip, ssm_norm, w_branch_a, w_branch_b, w_out, ffn_norm, w_ffn_up,
           w_ffn_conv, b_ffn_conv, w_ffn_down, ple_norm, w_ple_gate, w_ple_proj, final_norm):
    bsz, seq, _ = x.shape
    depth = w_in.shape[0]
    assert depth == 1, "the fused final RMSNorm assumes a single layer"
    x2 = x.reshape(bsz * seq, D_MODEL)
    out = _layer(x2, p[0].reshape(bsz * seq, PLE_DIM), bsz, seq, mixer_norm[0], w_in[0],
                 w_gla_gate[0], b_gla_gate[0], gla_norm[0], w_ssm_conv[0], b_ssm_conv[0],
                 dt_bias[0], a_log[0], d_skip[0], ssm_norm[0], w_branch_a[0], w_branch_b[0],
                 w_out[0], ffn_norm[0], w_ffn_up[0], w_ffn_conv[0], b_ffn_conv[0], w_ffn_down[0],
                 ple_norm[0], w_ple_gate[0], w_ple_proj[0], final_norm)
    return out.reshape(bsz, seq, D_MODEL)
```

```python
import functools

import jax
import jax.numpy as jnp
from jax import lax
from jax.experimental import pallas as pl
from jax.experimental.pallas import tpu as pltpu

f32 = jnp.float32
bf16 = jnp.bfloat16

D_MODEL = 1024
GLA_HEADS = 4
GLA_DK = 128
GLA_DV = 256
GLA_KEY = GLA_HEADS * GLA_DK
GLA_VAL = GLA_HEADS * GLA_DV
GLA_GATE_RANK = 16
GLA_GATE_NORM = 16.0
GLA_CHUNK = 64
SSM_INNER = 2 * D_MODEL
SSM_HEADDIM = 64
SSM_HEADS = SSM_INNER // SSM_HEADDIM
SSM_GROUPS = 4
SSM_HPG = SSM_HEADS // SSM_GROUPS
SSM_STATE = 128
SSM_CONV = 4
SSM_CHUNK = 128
SSM_BC = SSM_GROUPS * SSM_STATE
SSM_XBC = SSM_INNER + 2 * SSM_BC
SSM_GROUP_WIDTH = SSM_INNER // SSM_GROUPS
FFN_HIDDEN = 2816
FFN_CONV = 3
PLE_DIM = 256
EPS = 1e-6

LANES = 128
SUBLANES = 8
VMEM_LIMIT = 56 * 1024 * 1024

OFF_Z = 0
OFF_Q = OFF_Z + SSM_INNER
OFF_K = OFF_Q + GLA_KEY
OFF_V = OFF_K + GLA_KEY
OFF_GO = OFF_V + GLA_VAL
OFF_GA = OFF_GO + GLA_VAL
OFF_GB = OFF_GA + D_MODEL
MAIN_WIDTH = OFF_GB + D_MODEL
assert OFF_V % GLA_DV == 0 and OFF_GA % D_MODEL == 0
SM_ALR = 0
SM_DT = 32

TM_PROJ = 1024
TN_PROJ = 1024
TB_GLA = 1024
TB_SSD = 512
TM_MERGE = 1024
TM_FFN = 512
FFN_CB = 256


def _split3(x):
    hi = x.astype(bf16)
    r1 = x - hi.astype(f32)
    mid = r1.astype(bf16)
    lo = (r1 - mid.astype(f32)).astype(bf16)
    return hi, mid, lo


def _dot(a, b):
    return jnp.dot(a, b, preferred_element_type=f32)


def _dot_exact_lhs(m, x):
    hi, mid, lo = _split3(x)
    return _dot(m, hi) + _dot(m, mid) + _dot(m, lo)


def _dot_exact_rhs(x, m):
    hi, mid, lo = _split3(x)
    return _dot(hi, m) + _dot(mid, m) + _dot(lo, m)


def _dot_nt(a, b):
    return lax.dot_general(a, b, (((1,), (1,)), ((), ())), preferred_element_type=f32)


def _dot_tn(a, b):
    return lax.dot_general(a, b, (((0,), (0,)), ((), ())), preferred_element_type=f32)


def _rms(x, gain):
    return x * lax.rsqrt(jnp.mean(x * x, axis=-1, keepdims=True) + EPS) * gain


def _silu(x):
    h = 0.5 * x
    return h + h * jnp.tanh(h)


def _tri(n, dtype):
    r = lax.broadcasted_iota(jnp.int32, (n, n), 0)
    c = lax.broadcasted_iota(jnp.int32, (n, n), 1)
    return (r >= c).astype(dtype), r >= c


PROJ_HALO = 16
PROJ_CS = 256
PROJ_RC = 128
_CONV_BLOCKS = SSM_XBC // TN_PROJ
_N_BLOCKS = _CONV_BLOCKS + MAIN_WIDTH // TN_PROJ
_SILU_BLOCKS = tuple(_CONV_BLOCKS + b for b in
                     tuple(range(OFF_Z // TN_PROJ, (OFF_Z + SSM_INNER) // TN_PROJ))
                     + tuple(range(OFF_GO // TN_PROJ, (OFF_GO + GLA_VAL) // TN_PROJ)))
assert OFF_Z % TN_PROJ == 0 and OFF_GO % TN_PROJ == 0
assert SSM_XBC % TN_PROJ == 0 and SSM_INNER % TN_PROJ == 0 and GLA_VAL % TN_PROJ == 0


def _inproj_kernel(x_ref, xp_ref, g_ref, w_ref, ws_ref, cw_ref, cb_ref, o_ref, oc_ref, os_ref,
                   h_ref, *, tiles_per_seq):
    i = pl.program_id(0)
    j = pl.program_id(1)
    tm = x_ref.shape[0]

    @pl.when(j == 0)
    def _():
        hb = _rms(x_ref[...], g_ref[...]).astype(bf16)
        h_ref[PROJ_HALO:, :] = hb
        os_ref[...] = _dot(hb, ws_ref[...])
        hp = _rms(xp_ref[...], g_ref[...])
        first = (i % tiles_per_seq) == 0
        h_ref[0:PROJ_HALO, :] = jnp.where(first, 0.0, hp).astype(bf16)

    @pl.when(j < _CONV_BLOCKS)
    def _():
        n_rc = tm // PROJ_RC
        pieces = [(cs, rc) for cs in range(TN_PROJ // PROJ_CS) for rc in range(n_rc)]

        def raw(cs, rc):
            lo = 0 if rc == 0 else PROJ_HALO + rc * PROJ_RC
            return _dot(h_ref[lo:PROJ_HALO + (rc + 1) * PROJ_RC, :],
                        w_ref[:, cs * PROJ_CS:(cs + 1) * PROJ_CS])

        def conv(cs, rc, ext):
            cols = slice(cs * PROJ_CS, (cs + 1) * PROJ_CS)
            acc = cb_ref[:, cols] + cw_ref[SSM_CONV - 1:SSM_CONV, cols] * ext[SUBLANES:, :]
            for k in range(SSM_CONV - 1):
                shifted = pltpu.roll(ext, SSM_CONV - 1 - k, 0)[SUBLANES:, :]
                acc = acc + cw_ref[k:k + 1, cols] * shifted
            oc_ref[rc * PROJ_RC:(rc + 1) * PROJ_RC, cols] = _silu(acc).astype(bf16)

        r = raw(*pieces[0])
        tail = None
        for n, (cs, rc) in enumerate(pieces):
            r_next = raw(*pieces[n + 1]) if n + 1 < len(pieces) else None
            ext = r[PROJ_HALO - SUBLANES:, :] if rc == 0 else jnp.concatenate([tail, r], axis=0)
            tail = r[r.shape[0] - SUBLANES:, :]
            conv(cs, rc, ext)
            r = r_next

    is_silu = functools.reduce(lambda a, b: a | b, [j == b for b in _SILU_BLOCKS])

    @pl.when(is_silu)
    def _():
        o_ref[...] = _silu(_dot(h_ref[PROJ_HALO:, :], w_ref[...])).astype(bf16)

    @pl.when((j >= _CONV_BLOCKS) & jnp.logical_not(is_silu))
    def _():
        o_ref[...] = _dot(h_ref[PROJ_HALO:, :], w_ref[...]).astype(bf16)


def _inproj(x2, gain, w_all, w_small, cw, cb, seq):
    n_tok = x2.shape[0]
    tm = min(TM_PROJ, seq)
    grid = (n_tok // tm, _N_BLOCKS)
    halo_blocks = tm // PROJ_HALO

    def conv_block(i, j):
        return (0, jnp.minimum(j, _CONV_BLOCKS - 1))

    return pl.pallas_call(
        functools.partial(_inproj_kernel, tiles_per_seq=seq // tm),
        grid=grid,
        in_specs=[
            pl.BlockSpec((tm, D_MODEL), lambda i, j: (i, 0)),
            pl.BlockSpec((PROJ_HALO, D_MODEL), lambda i, j: (jnp.maximum(i * halo_blocks - 1, 0), 0)),
            pl.BlockSpec((1, D_MODEL), lambda i, j: (0, 0)),
            pl.BlockSpec((D_MODEL, TN_PROJ), lambda i, j: (0, j)),
            pl.BlockSpec((D_MODEL, LANES), lambda i, j: (0, 0)),
            pl.BlockSpec((SSM_CONV, TN_PROJ), conv_block),
            pl.BlockSpec((1, TN_PROJ), conv_block),
        ],
        out_specs=[
            pl.BlockSpec((tm, TN_PROJ), lambda i, j: (i, jnp.maximum(j - _CONV_BLOCKS, 0))),
            pl.BlockSpec((tm, TN_PROJ), lambda i, j: (i, jnp.minimum(j, _CONV_BLOCKS - 1))),
            pl.BlockSpec((tm, LANES), lambda i, j: (i, 0)),
        ],
        out_shape=[
            jax.ShapeDtypeStruct((n_tok, MAIN_WIDTH), bf16),
            jax.ShapeDtypeStruct((n_tok, SSM_XBC), bf16),
            jax.ShapeDtypeStruct((n_tok, LANES), f32),
        ],
        scratch_shapes=[pltpu.VMEM((PROJ_HALO + tm, D_MODEL), bf16)],
        compiler_params=pltpu.CompilerParams(
            dimension_semantics=("arbitrary", "arbitrary"),
            vmem_limit_bytes=VMEM_LIMIT),
        name="inproj",
    )(x2, x2, gain, w_all, w_small, cw, cb)


def _gla_kernel(q_ref, k_ref, v_ref, go_ref, sm_ref, wg_ref, bg_ref, gn_ref,
                o_ref, st_ref, gl_ref):
    C = GLA_CHUNK

    @pl.when(pl.program_id(2) == 0)
    def _():
        st_ref[...] = jnp.zeros_like(st_ref)

    a_lr = sm_ref[:, SM_ALR:SM_ALR + GLA_GATE_RANK].astype(bf16)
    logit = _dot(a_lr, wg_ref[...].astype(bf16)) + bg_ref[...]
    gl_ref[...] = (jnp.minimum(logit, 0.0)
                   - jnp.log1p(jnp.exp(-jnp.abs(logit)))) * (1.0 / GLA_GATE_NORM)

    tri, causal = _tri(C, bf16)
    gain = gn_ref[...]
    n_chunks = q_ref.shape[0] // C

    chunks = [slice(c * C, (c + 1) * C) for c in range(n_chunks)]
    b3s = []
    for rows in chunks:
        hi, mid, lo = _split3(gl_ref[rows, :])
        b3s.append(_dot(tri, jnp.concatenate([hi, mid, lo], axis=1)))
    q_tb, k_tb, k_db, decay = [], [], [], []
    for rows, b3 in zip(chunks, b3s):
        b = (b3[:, 0:GLA_DK] + b3[:, GLA_DK:2 * GLA_DK]) + b3[:, 2 * GLA_DK:3 * GLA_DK]
        b_last = b[C - 1:C, :]
        kf = k_ref[rows, :].astype(f32)
        q_tb.append(((q_ref[rows, :].astype(f32) * (GLA_DK ** -0.5)) * jnp.exp(b)).astype(bf16))
        k_tb.append((kf * jnp.exp(-b)).astype(bf16))
        k_db.append((kf * jnp.exp(b_last - b)).astype(bf16))
        decay.append(jnp.exp(b_last))
    att = [_dot_nt(q_tb[c], k_tb[c]) for c in range(n_chunks)]
    upd = [_dot_tn(v_ref[chunks[c], :], k_db[c]) for c in range(n_chunks)]
    o_intra = [_dot(jnp.where(causal, att[c], 0.0).astype(bf16), v_ref[chunks[c], :])
               for c in range(n_chunks)]
    st = st_ref[...]
    for c, rows in enumerate(chunks):
        o = o_intra[c] + _dot_nt(q_tb[c], st.astype(bf16))
        st = decay[c] * st + upd[c]
        o = _rms(o, gain) * go_ref[rows, :].astype(f32)
        o_ref[rows, :] = o.astype(bf16)
    st_ref[...] = st


def _gla(proj, small, w_gate, b_gate, gla_norm, bsz, seq):
    n_tok = bsz * seq
    tb = min(TB_GLA, seq)
    n_t = seq // tb

    def row(b, h, t):
        return b * n_t + t

    return pl.pallas_call(
        _gla_kernel,
        grid=(bsz, GLA_HEADS, n_t),
        in_specs=[
            pl.BlockSpec((tb, GLA_DK), lambda b, h, t: (row(b, h, t), OFF_Q // GLA_DK + h)),
            pl.BlockSpec((tb, GLA_DK), lambda b, h, t: (row(b, h, t), OFF_K // GLA_DK + h)),
            pl.BlockSpec((tb, GLA_DV), lambda b, h, t: (row(b, h, t), OFF_V // GLA_DV + h)),
            pl.BlockSpec((tb, GLA_DV), lambda b, h, t: (row(b, h, t), OFF_GO // GLA_DV + h)),
            pl.BlockSpec((tb, LANES), lambda b, h, t: (row(b, h, t), 0)),
            pl.BlockSpec((GLA_GATE_RANK, GLA_DK), lambda b, h, t: (0, h)),
            pl.BlockSpec((1, GLA_DK), lambda b, h, t: (0, h)),
            pl.BlockSpec((1, GLA_DV), lambda b, h, t: (0, 0)),
        ],
        out_specs=pl.BlockSpec((tb, GLA_DV), lambda b, h, t: (row(b, h, t), h)),
        out_shape=jax.ShapeDtypeStruct((n_tok, GLA_VAL), bf16),
        scratch_shapes=[pltpu.VMEM((GLA_DV, GLA_DK), f32),
                        pltpu.VMEM((tb, GLA_DK), f32)],
        compiler_params=pltpu.CompilerParams(
            dimension_semantics=("arbitrary", "arbitrary", "arbitrary"),
            vmem_limit_bytes=VMEM_LIMIT),
        name="gla",
    )(proj, proj, proj, proj, small, w_gate, b_gate, gla_norm)


def _ssd_kernel(z_ref, xc_ref, sm_ref, dtb_ref, arow_ref, dsk_ref, nrm_ref, e_ref,
                o_ref, st_ref):
    tb = xc_ref.shape[0]
    L = SSM_CHUNK

    @pl.when(pl.program_id(1) == 0)
    def _():
        st_ref[...] = jnp.zeros_like(st_ref)

    tri, causal = _tri(L, bf16)
    lane = lax.broadcasted_iota(jnp.int32, (L, LANES), 1)
    left = lane < SSM_HEADDIM

    n_chunks = tb // L
    chunks = [slice(c * L, (c + 1) * L) for c in range(n_chunks)]
    groups = [slice(g * SSM_GROUP_WIDTH, (g + 1) * SSM_GROUP_WIDTH) for g in range(SSM_GROUPS)]
    CD = 2 * SUBLANES

    dts = [jax.nn.softplus(sm_ref[rows, :] + dtb_ref[...]) for rows in chunks]
    parts = []
    for dt in dts:
        parts.extend(_split3(dt * arow_ref[...]))
    cs_all = _dot(tri, jnp.concatenate(parts, axis=1))
    a_cs, a_cs_t, lhs_main, lhs_cd = [], [], [], []
    for c, dt in enumerate(dts):
        o3 = 3 * LANES * c
        acs = (cs_all[:, o3:o3 + LANES] + cs_all[:, o3 + LANES:o3 + 2 * LANES]) \
            + cs_all[:, o3 + 2 * LANES:o3 + 3 * LANES]
        a_last = acs[L - 1:L, :]
        a_cs.append(acs)
        a_cs_t.append(acs.T)
        lhs_main += [dt.astype(bf16), (dt * jnp.exp(a_last - acs)).astype(bf16),
                     jnp.exp(acs).astype(bf16)]
        lhs_cd += list(_split3(jnp.broadcast_to(jnp.exp(a_last), (CD, LANES))))
    lhs = jnp.concatenate(lhs_main + lhs_cd, axis=0)

    exp_g = [_dot(lhs, e_ref[g]) for g in range(SSM_GROUPS)]

    def expanded(g, c, k):
        r0 = (3 * c + k) * L
        return exp_g[g][r0:r0 + L, :]

    def chunk_decay(g, c):
        r0 = 3 * L * n_chunks + 3 * CD * c
        r = exp_g[g]
        return (r[r0:r0 + 1, :] + r[r0 + CD:r0 + CD + 1, :]) + r[r0 + 2 * CD:r0 + 2 * CD + 1, :]

    pairs = [(c, g) for c in range(n_chunks) for g in range(SSM_GROUPS)]
    bm, cm, xg, scores, xdt, upd = {}, {}, {}, {}, {}, {}
    for c, g in pairs:
        rows = chunks[c]
        bm[c, g] = xc_ref[rows, SSM_INNER + g * SSM_STATE:SSM_INNER + (g + 1) * SSM_STATE]
        cm[c, g] = xc_ref[rows, SSM_INNER + SSM_BC + g * SSM_STATE:
                          SSM_INNER + SSM_BC + (g + 1) * SSM_STATE]
        xg[c, g] = xc_ref[rows, groups[g]].astype(f32)
        scores[c, g] = _dot_nt(cm[c, g], bm[c, g])
        xdt[c, g] = xg[c, g] * expanded(g, c, 0)
        upd[c, g] = _dot_tn(bm[c, g], (xg[c, g] * expanded(g, c, 1)).astype(bf16))

    y_diag = {}
    for c, g in pairs:
        tiles = []
        for qd in range(SSM_HPG // 2):
            ms = []
            for j in (2 * qd, 2 * qd + 1):
                ln = SM_DT + g * SSM_HPG + j
                col = jnp.broadcast_to(a_cs[c][:, ln:ln + 1], (L, L))
                rw = jnp.broadcast_to(a_cs_t[c][ln:ln + 1, :], (L, L))
                ldec = jnp.exp(jnp.where(causal, col - rw, -jnp.inf))
                ms.append((scores[c, g] * ldec).astype(bf16))
            xt = xdt[c, g][:, qd * LANES:(qd + 1) * LANES]
            xcat = jnp.concatenate([jnp.where(left, xt, 0.0), jnp.where(left, 0.0, xt)],
                                   axis=0).astype(bf16)
            tiles.append(_dot(jnp.concatenate(ms, axis=1), xcat))
        y_diag[c, g] = jnp.concatenate(tiles, axis=1)

    for g in range(SSM_GROUPS):
        st = st_ref[g]
        for c in range(n_chunks):
            rows = chunks[c]
            y = (y_diag[c, g] + _dot(cm[c, g], st.astype(bf16)) * expanded(g, c, 2)
                 + xg[c, g] * dsk_ref[:, groups[g]])
            st = st * chunk_decay(g, c) + upd[c, g]
            y = y * z_ref[rows, groups[g]].astype(f32)
            y = _rms(y, nrm_ref[:, groups[g]])
            o_ref[rows, groups[g]] = y.astype(bf16)
        st_ref[g] = st


def _ssd(proj, xbc, small, dtb_row, a_row, dsk_row, nrm, e_mat, bsz, seq):
    n_tok = bsz * seq
    tb = min(TB_SSD, seq)
    n_t = seq // tb

    def row(b, t):
        return b * n_t + t

    const2 = lambda b, t: (0, 0)
    return pl.pallas_call(
        _ssd_kernel,
        grid=(bsz, n_t),
        in_specs=[
            pl.BlockSpec((tb, SSM_INNER), lambda b, t: (row(b, t), OFF_Z // SSM_INNER)),
            pl.BlockSpec((tb, SSM_XBC), lambda b, t: (row(b, t), 0)),
            pl.BlockSpec((tb, LANES), lambda b, t: (row(b, t), 0)),
            pl.BlockSpec((1, LANES), const2),
            pl.BlockSpec((1, LANES), const2),
            pl.BlockSpec((1, SSM_INNER), const2),
            pl.BlockSpec((1, SSM_INNER), const2),
            pl.BlockSpec((SSM_GROUPS, LANES, SSM_GROUP_WIDTH), lambda b, t: (0, 0, 0)),
        ],
        out_specs=pl.BlockSpec((tb, SSM_INNER), lambda b, t: (row(b, t), 0)),
        out_shape=jax.ShapeDtypeStruct((n_tok, SSM_INNER), bf16),
        scratch_shapes=[pltpu.VMEM((SSM_GROUPS, SSM_STATE, SSM_GROUP_WIDTH), f32)],
        compiler_params=pltpu.CompilerParams(
            dimension_semantics=("arbitrary", "arbitrary"),
            vmem_limit_bytes=VMEM_LIMIT),
        name="ssd",
    )(proj, xbc, small, dtb_row, a_row, dsk_row, nrm, e_mat)


def _merge_kernel(oa_ref, ob_ref, ga_ref, gb_ref, x_ref, wa_ref, wb_ref, wo_ref, o_ref):
    ma = _dot(oa_ref[...], wa_ref[...])
    mb = _dot(ob_ref[...], wb_ref[...])
    merged = (jax.nn.sigmoid(ga_ref[...].astype(f32)) * ma
              + jax.nn.sigmoid(gb_ref[...].astype(f32)) * mb)
    o_ref[...] = x_ref[...] + _dot(merged.astype(bf16), wo_ref[...])


def _resident(shape):
    nd = len(shape)
    return pl.BlockSpec(shape, lambda *_: (0,) * nd, pipeline_mode=pl.Buffered(1))


def _merge(o_a, o_b, proj, x2, wa, wb, wo):
    n_tok = x2.shape[0]
    tm = min(TM_MERGE, n_tok)
    return pl.pallas_call(
        _merge_kernel,
        grid=(n_tok // tm,),
        in_specs=[
            pl.BlockSpec((tm, GLA_VAL), lambda i: (i, 0)),
            pl.BlockSpec((tm, SSM_INNER), lambda i: (i, 0)),
            pl.BlockSpec((tm, D_MODEL), lambda i: (i, OFF_GA // D_MODEL)),
            pl.BlockSpec((tm, D_MODEL), lambda i: (i, OFF_GB // D_MODEL)),
            pl.BlockSpec((tm, D_MODEL), lambda i: (i, 0)),
            _resident((GLA_VAL, D_MODEL)),
            _resident((SSM_INNER, D_MODEL)),
            _resident((D_MODEL, D_MODEL)),
        ],
        out_specs=pl.BlockSpec((tm, D_MODEL), lambda i: (i, 0)),
        out_shape=jax.ShapeDtypeStruct((n_tok, D_MODEL), f32),
        compiler_params=pltpu.CompilerParams(
            dimension_semantics=("arbitrary",),
            vmem_limit_bytes=VMEM_LIMIT),
        name="merge",
    )(o_a, o_b, proj, proj, x2, wa, wb, wo)


def _ffn_kernel(x_ref, p_ref, fn_ref, wup_ref, cw_ref, cb_ref, wdn_ref, pn_ref,
                wpg_ref, wpp_ref, fin_ref, o_ref, g_ref, halo_ref, ae_ref):
    tm = x_ref.shape[0]
    H = SUBLANES

    @pl.when(pl.program_id(1) == 0)
    def _():
        halo_ref[...] = jnp.zeros_like(halo_ref)

    x1 = x_ref[...]
    h = _rms(x1, fn_ref[...]).astype(bf16)
    for c in range(FFN_HIDDEN // FFN_CB):
        cols = slice(c * FFN_CB, (c + 1) * FFN_CB)
        act = _dot(h, wup_ref[:, cols])
        lin = _dot(h, wup_ref[:, FFN_HIDDEN + c * FFN_CB:FFN_HIDDEN + (c + 1) * FFN_CB])
        ae_ref[0:H, :] = halo_ref[:, cols]
        ae_ref[H:H + tm, :] = act
        halo_ref[:, cols] = act[tm - H:tm, :]
        conv = cb_ref[:, cols]
        for k in range(FFN_CONV):
            conv = conv + cw_ref[k:k + 1, cols] * ae_ref[pl.ds(H - (FFN_CONV - 1) + k, tm), :]
        g_ref[:, cols] = (jax.nn.gelu(conv) * lin).astype(bf16)

    x2 = x1 + _dot(g_ref[...], wdn_ref[...])
    gate = jax.nn.sigmoid(_dot(_rms(x2, pn_ref[...]).astype(bf16), wpg_ref[...]))
    x3 = x2 + gate * _dot(p_ref[...].astype(bf16), wpp_ref[...])
    o_ref[...] = _rms(x3, fin_ref[...])


def _ffn(x1, p2, ffn_norm, wup, cw, cb, wdn, ple_norm, wpg, wpp, final_norm, bsz, seq):
    n_tok = bsz * seq
    tm = min(TM_FFN, seq)
    n_t = seq // tm
    rowmap = lambda b, t: (b * n_t + t, 0)
    return pl.pallas_call(
        _ffn_kernel,
        grid=(bsz, n_t),
        in_specs=[
            pl.BlockSpec((tm, D_MODEL), rowmap),
            pl.BlockSpec((tm, PLE_DIM), rowmap),
            _resident((1, D_MODEL)),
            _resident((D_MODEL, 2 * FFN_HIDDEN)),
            _resident((FFN_CONV, FFN_HIDDEN)),
            _resident((1, FFN_HIDDEN)),
            _resident((FFN_HIDDEN, D_MODEL)),
            _resident((1, D_MODEL)),
            _resident((D_MODEL, D_MODEL)),
            _resident((PLE_DIM, D_MODEL)),
            _resident((1, D_MODEL)),
        ],
        out_specs=pl.BlockSpec((tm, D_MODEL), rowmap),
        out_shape=jax.ShapeDtypeStruct((n_tok, D_MODEL), f32),
        scratch_shapes=[
            pltpu.VMEM((tm, FFN_HIDDEN), bf16),
            pltpu.VMEM((SUBLANES, FFN_HIDDEN), f32),
            pltpu.VMEM((tm + SUBLANES, FFN_CB), f32),
        ],
        compiler_params=pltpu.CompilerParams(
            dimension_semantics=("arbitrary", "arbitrary"),
            vmem_limit_bytes=VMEM_LIMIT),
        name="ffn",
    )(x1, p2, ffn_norm, wup, cw, cb, wdn, ple_norm, wpg, wpp, final_norm)


_IN_SIZES = (GLA_KEY, GLA_KEY, GLA_VAL, GLA_VAL, GLA_GATE_RANK, SSM_INNER, SSM_XBC, SSM_HEADS,
             D_MODEL, D_MODEL)
_IN_OFFS = tuple(sum(_IN_SIZES[:n]) for n in range(len(_IN_SIZES) + 1))
_IN_WIDTH = _IN_OFFS[-1]
_REPACK_SRC = (tuple(_IN_OFFS[6] + TN_PROJ * n for n in range(SSM_XBC // TN_PROJ))
               + tuple(_IN_OFFS[5] + TN_PROJ * n for n in range(SSM_INNER // TN_PROJ))
               + (_IN_OFFS[0],)
               + tuple(_IN_OFFS[2] + TN_PROJ * n for n in range(GLA_VAL // TN_PROJ))
               + tuple(_IN_OFFS[3] + TN_PROJ * n for n in range(GLA_VAL // TN_PROJ))
               + (_IN_OFFS[8], _IN_OFFS[9]))
assert len(_REPACK_SRC) == _N_BLOCKS and 2 * GLA_KEY == TN_PROJ
_REPACK_WIN = TN_PROJ // LANES + 1


def _repack_kernel(*refs):
    win_refs, o_ref = refs[:_REPACK_WIN], refs[_REPACK_WIN]
    b = pl.program_id(0)
    for shift in sorted(set(s % LANES for s in _REPACK_SRC)):
        hit = functools.reduce(lambda a, c: a | c,
                               [b == n for n, s in enumerate(_REPACK_SRC) if s % LANES == shift])

        @pl.when(hit)
        def _(shift=shift):
            wide = jnp.concatenate([r[...] for r in win_refs], axis=1)
            o_ref[...] = wide[:, shift:shift + TN_PROJ].astype(bf16)


def _repack(w_in):
    def window(k):
        def index(b):
            base = functools.reduce(lambda acc, ns: jnp.where(b == ns[0], ns[1] // LANES, acc),
                                    list(enumerate(_REPACK_SRC)), 0)
            return (0, base + k)
        return pl.BlockSpec((D_MODEL, LANES), index)

    return pl.pallas_call(
        _repack_kernel,
        grid=(_N_BLOCKS,),
        in_specs=[window(k) for k in range(_REPACK_WIN)],
        out_specs=pl.BlockSpec((D_MODEL, TN_PROJ), lambda b: (0, b)),
        out_shape=jax.ShapeDtypeStruct((D_MODEL, _N_BLOCKS * TN_PROJ), bf16),
        compiler_params=pltpu.CompilerParams(dimension_semantics=("arbitrary",)),
        name="repack",
    )(*([w_in] * _REPACK_WIN))


def _lane_row(vals, offset):
    return jnp.zeros((1, LANES), f32).at[0, offset:offset + vals.shape[0]].set(vals.astype(f32))


def _expand_mats():
    src = jnp.arange(LANES)[None, :, None]
    dst = jnp.arange(SSM_GROUP_WIDTH)[None, None, :]
    g = jnp.arange(SSM_GROUPS)[:, None, None]
    return (src == SM_DT + g * SSM_HPG + dst // SSM_HEADDIM).astype(bf16)


def _layer(x2, p2, bsz, seq, mixer_norm, w_in, w_gla_gate, b_gla_gate, gla_norm, w_ssm_conv,
           b_ssm_conv, dt_bias, a_log, d_skip, ssm_norm, w_branch_a, w_branch_b, w_out,
           ffn_norm, w_ffn_up, w_ffn_conv, b_ffn_conv, w_ffn_down, ple_norm, w_ple_gate,
           w_ple_proj, out_norm):
    assert w_in.shape == (D_MODEL, _IN_WIDTH)
    w_all = _repack(w_in)
    w_small = jnp.zeros((D_MODEL, LANES), f32)
    w_small = w_small.at[:, SM_ALR:SM_ALR + GLA_GATE_RANK].set(w_in[:, _IN_OFFS[4]:_IN_OFFS[5]])
    w_small = w_small.at[:, SM_DT:SM_DT + SSM_HEADS].set(w_in[:, _IN_OFFS[7]:_IN_OFFS[8]]).astype(bf16)

    proj, xbc, small = _inproj(x2, mixer_norm[None, :], w_all, w_small, w_ssm_conv,
                               b_ssm_conv[None, :], seq)

    o_a = _gla(proj, small, w_gla_gate, b_gla_gate[None, :], gla_norm[None, :], bsz, seq)

    o_b = _ssd(proj, xbc, small,
               _lane_row(dt_bias, SM_DT), _lane_row(-jnp.exp(a_log.astype(f32)), SM_DT),
               jnp.repeat(d_skip.astype(f32), SSM_HEADDIM)[None, :], ssm_norm[None, :],
               _expand_mats(), bsz, seq)

    x1 = _merge(o_a, o_b, proj, x2, w_branch_a.astype(bf16), w_branch_b.astype(bf16),
                w_out.astype(bf16))

    return _ffn(x1, p2, ffn_norm[None, :], w_ffn_up.astype(bf16), w_ffn_conv, b_ffn_conv[None, :],
                w_ffn_down.astype(bf16), ple_norm[None, :], w_ple_gate.astype(bf16),
                w_ple_proj.astype(bf16), out_norm[None, :], bsz, seq)


def kernel(x, p, mixer_norm, w_in, w_gla_gate, b_gla_gate, gla_norm, w_ssm_conv, b_ssm_conv,
           dt_bias, a_log, d_skip, ssm_norm, w_branch_a, w_branch_b, w_out, ffn_norm, w_ffn_up,
           w_ffn_conv, b_ffn_conv, w_ffn_down, ple_norm, w_ple_gate, w_ple_proj, final_norm):
    bsz, seq, _ = x.shape
    depth = w_in.shape[0]
    assert depth == 1, "the fused final RMSNorm assumes a single layer"
    x2 = x.reshape(bsz * seq, D_MODEL)
    out = _layer(x2, p[0].reshape(bsz * seq, PLE_DIM), bsz, seq, mixer_norm[0], w_in[0],
                 w_gla_gate[0], b_gla_gate[0], gla_norm[0], w_ssm_conv[0], b_ssm_conv[0],
                 dt_bias[0], a_log[0], d_skip[0], ssm_norm[0], w_branch_a[0], w_branch_b[0],
                 w_out[0], ffn_norm[0], w_ffn_up[0], w_ffn_conv[0], b_ffn_conv[0], w_ffn_down[0],
                 ple_norm[0], w_ple_gate[0], w_ple_proj[0], final_norm)
    return out.reshape(bsz, seq, D_MODEL)
```

```python
import functools

import jax
import jax.numpy as jnp
from jax import lax
from jax.experimental import pallas as pl
from jax.experimental.pallas import tpu as pltpu

f32 = jnp.float32
bf16 = jnp.bfloat16

D_MODEL = 1024
GLA_HEADS = 4
GLA_DK = 128
GLA_DV = 256
GLA_KEY = GLA_HEADS * GLA_DK
GLA_VAL = GLA_HEADS * GLA_DV
GLA_GATE_RANK = 16
GLA_GATE_NORM = 16.0
GLA_CHUNK = 64
SSM_INNER = 2 * D_MODEL
SSM_HEADDIM = 64
SSM_HEADS = SSM_INNER // SSM_HEADDIM
SSM_GROUPS = 4
SSM_HPG = SSM_HEADS // SSM_GROUPS
SSM_STATE = 128
SSM_CONV = 4
SSM_CHUNK = 128
SSM_BC = SSM_GROUPS * SSM_STATE
SSM_XBC = SSM_INNER + 2 * SSM_BC
SSM_GROUP_WIDTH = SSM_INNER // SSM_GROUPS
FFN_HIDDEN = 2816
FFN_CONV = 3
PLE_DIM = 256
EPS = 1e-6

LANES = 128
SUBLANES = 8
VMEM_LIMIT = 56 * 1024 * 1024

OFF_Z = 0
OFF_Q = OFF_Z + SSM_INNER
OFF_K = OFF_Q + GLA_KEY
OFF_V = OFF_K + GLA_KEY
OFF_GO = OFF_V + GLA_VAL
OFF_GA = OFF_GO + GLA_VAL
OFF_GB = OFF_GA + D_MODEL
MAIN_WIDTH = OFF_GB + D_MODEL
assert OFF_V % GLA_DV == 0 and OFF_GA % D_MODEL == 0
SM_ALR = 0
SM_DT = 32

TM_PROJ = 2048
TN_PROJ = 1024
TB_GLA = 2048
TB_SSD = 512
TM_MERGE = 1024
TM_FFN = 512
FFN_CB = 256


def _split3(x):
    hi = x.astype(bf16)
    r1 = x - hi.astype(f32)
    mid = r1.astype(bf16)
    lo = (r1 - mid.astype(f32)).astype(bf16)
    return hi, mid, lo


def _dot(a, b):
    return jnp.dot(a, b, preferred_element_type=f32)


def _dot_nt(a, b):
    return lax.dot_general(a, b, (((1,), (1,)), ((), ())), preferred_element_type=f32)


def _dot_tn(a, b):
    return lax.dot_general(a, b, (((0,), (0,)), ((), ())), preferred_element_type=f32)


def _rms(x, gain):
    return x * lax.rsqrt(jnp.mean(x * x, axis=-1, keepdims=True) + EPS) * gain


def _silu(x):
    h = 0.5 * x
    return h + h * jnp.tanh(h)


def _tri(n, dtype):
    r = lax.broadcasted_iota(jnp.int32, (n, n), 0)
    c = lax.broadcasted_iota(jnp.int32, (n, n), 1)
    return (r >= c).astype(dtype), r >= c


PROJ_HALO = 16
PROJ_CS = 256
PROJ_RC = TM_PROJ
_CONV_BLOCKS = SSM_XBC // TN_PROJ
_N_BLOCKS = _CONV_BLOCKS + MAIN_WIDTH // TN_PROJ
_SILU_BLOCKS = tuple(_CONV_BLOCKS + b for b in
                     tuple(range(OFF_Z // TN_PROJ, (OFF_Z + SSM_INNER) // TN_PROJ))
                     + tuple(range(OFF_GO // TN_PROJ, (OFF_GO + GLA_VAL) // TN_PROJ)))
assert OFF_Z % TN_PROJ == 0 and OFF_GO % TN_PROJ == 0
assert SSM_XBC % TN_PROJ == 0 and SSM_INNER % TN_PROJ == 0 and GLA_VAL % TN_PROJ == 0


def _inproj_kernel(x_ref, xp_ref, g_ref, w_ref, ws_ref, cw_ref, cb_ref, o_ref, oc_ref, os_ref,
                   h_ref, *, tiles_per_seq):
    i = pl.program_id(0)
    j = pl.program_id(1)
    tm = x_ref.shape[0]

    @pl.when(j == 0)
    def _():
        hb = _rms(x_ref[...], g_ref[...]).astype(bf16)
        h_ref[PROJ_HALO:, :] = hb
        os_ref[...] = _dot_nt(hb, ws_ref[...])
        hp = _rms(xp_ref[...], g_ref[...])
        first = (i % tiles_per_seq) == 0
        h_ref[0:PROJ_HALO, :] = jnp.where(first, 0.0, hp).astype(bf16)

    @pl.when(j < _CONV_BLOCKS)
    def _():
        rows = min(PROJ_RC, tm)
        n_rc = tm // rows
        pieces = [(cs, rc) for cs in range(TN_PROJ // PROJ_CS) for rc in range(n_rc)]

        def raw(cs, rc):
            lo = 0 if rc == 0 else PROJ_HALO + rc * rows
            return _dot_nt(h_ref[lo:PROJ_HALO + (rc + 1) * rows, :],
                           w_ref[cs * PROJ_CS:(cs + 1) * PROJ_CS, :])

        def conv(cs, rc, ext):
            cols = slice(cs * PROJ_CS, (cs + 1) * PROJ_CS)
            taps = 0.5 * cw_ref[:, cols]
            acc = 0.5 * cb_ref[:, cols] + taps[SSM_CONV - 1:SSM_CONV, :] * ext[SUBLANES:, :]
            for k in range(SSM_CONV - 1):
                shifted = pltpu.roll(ext, SSM_CONV - 1 - k, 0)[SUBLANES:, :]
                acc = acc + taps[k:k + 1, :] * shifted
            oc_ref[rc * rows:(rc + 1) * rows, cols] = (acc + acc * jnp.tanh(acc)).astype(bf16)

        r = raw(*pieces[0])
        tail = None
        for n, (cs, rc) in enumerate(pieces):
            r_next = raw(*pieces[n + 1]) if n + 1 < len(pieces) else None
            ext = r[PROJ_HALO - SUBLANES:, :] if rc == 0 else jnp.concatenate([tail, r], axis=0)
            tail = r[r.shape[0] - SUBLANES:, :]
            conv(cs, rc, ext)
            r = r_next

    is_silu = functools.reduce(lambda a, b: a | b, [j == b for b in _SILU_BLOCKS])

    @pl.when(is_silu)
    def _():
        o_ref[...] = _silu(_dot_nt(h_ref[PROJ_HALO:, :], w_ref[...])).astype(bf16)

    @pl.when((j >= _CONV_BLOCKS) & jnp.logical_not(is_silu))
    def _():
        o_ref[...] = _dot_nt(h_ref[PROJ_HALO:, :], w_ref[...]).astype(bf16)


def _inproj(x2, gain, w_all, w_small, cw, cb, seq):
    n_tok = x2.shape[0]
    tm = min(TM_PROJ, seq)
    grid = (n_tok // tm, _N_BLOCKS)
    halo_blocks = tm // PROJ_HALO

    def conv_block(i, j):
        return (0, jnp.minimum(j, _CONV_BLOCKS - 1))

    return pl.pallas_call(
        functools.partial(_inproj_kernel, tiles_per_seq=seq // tm),
        grid=grid,
        in_specs=[
            pl.BlockSpec((tm, D_MODEL), lambda i, j: (i, 0)),
            pl.BlockSpec((PROJ_HALO, D_MODEL), lambda i, j: (jnp.maximum(i * halo_blocks - 1, 0), 0)),
            pl.BlockSpec((1, D_MODEL), lambda i, j: (0, 0)),
            pl.BlockSpec((TN_PROJ, D_MODEL), lambda i, j: (j, 0)),
            pl.BlockSpec((LANES, D_MODEL), lambda i, j: (0, 0)),
            pl.BlockSpec((SSM_CONV, TN_PROJ), conv_block),
            pl.BlockSpec((1, TN_PROJ), conv_block),
        ],
        out_specs=[
            pl.BlockSpec((tm, TN_PROJ), lambda i, j: (i, jnp.maximum(j - _CONV_BLOCKS, 0))),
            pl.BlockSpec((tm, TN_PROJ), lambda i, j: (i, jnp.minimum(j, _CONV_BLOCKS - 1))),
            pl.BlockSpec((tm, LANES), lambda i, j: (i, 0)),
        ],
        out_shape=[
            jax.ShapeDtypeStruct((n_tok, MAIN_WIDTH), bf16),
            jax.ShapeDtypeStruct((n_tok, SSM_XBC), bf16),
            jax.ShapeDtypeStruct((n_tok, LANES), f32),
        ],
        scratch_shapes=[pltpu.VMEM((PROJ_HALO + tm, D_MODEL), bf16)],
        compiler_params=pltpu.CompilerParams(
            dimension_semantics=("arbitrary", "arbitrary"),
            vmem_limit_bytes=VMEM_LIMIT),
        name="inproj",
    )(x2, x2, gain, w_all, w_small, cw, cb)


def _gla_kernel(q_ref, k_ref, v_ref, go_ref, sm_ref, wg_ref, bg_ref, gn_ref,
                o_ref, st_ref, gl_ref):
    C = GLA_CHUNK

    @pl.when(pl.program_id(2) == 0)
    def _():
        st_ref[...] = jnp.zeros_like(st_ref)

    a_lr = sm_ref[:, SM_ALR:SM_ALR + GLA_GATE_RANK].astype(bf16)
    logit = _dot(a_lr, wg_ref[...].astype(bf16)) + bg_ref[...]
    gl_ref[...] = (jnp.minimum(logit, 0.0)
                   - jnp.log1p(jnp.exp(-jnp.abs(logit)))) * (1.0 / GLA_GATE_NORM)

    tri, causal = _tri(C, bf16)
    gain = gn_ref[...]
    n_chunks = q_ref.shape[0] // C

    chunks = [slice(c * C, (c + 1) * C) for c in range(n_chunks)]
    b3s = []
    for rows in chunks:
        hi, mid, lo = _split3(gl_ref[rows, :])
        b3s.append(_dot(tri, jnp.concatenate([hi, mid, lo], axis=1)))
    q_tb, k_tb, k_db, decay = [], [], [], []
    for rows, b3 in zip(chunks, b3s):
        b = (b3[:, 0:GLA_DK] + b3[:, GLA_DK:2 * GLA_DK]) + b3[:, 2 * GLA_DK:3 * GLA_DK]
        b_last = b[C - 1:C, :]
        kf = k_ref[rows, :].astype(f32)
        q_tb.append(((q_ref[rows, :].astype(f32) * (GLA_DK ** -0.5)) * jnp.exp(b)).astype(bf16))
        k_tb.append((kf * jnp.exp(-b)).astype(bf16))
        k_db.append((kf * jnp.exp(b_last - b)).astype(bf16))
        decay.append(jnp.exp(b_last))
    att = [_dot_nt(q_tb[c], k_tb[c]) for c in range(n_chunks)]
    upd = [_dot_tn(v_ref[chunks[c], :], k_db[c]) for c in range(n_chunks)]
    o_intra = [_dot(jnp.where(causal, att[c], 0.0).astype(bf16), v_ref[chunks[c], :])
               for c in range(n_chunks)]
    st = st_ref[...]
    for c, rows in enumerate(chunks):
        o = o_intra[c] + _dot_nt(q_tb[c], st.astype(bf16))
        st = decay[c] * st + upd[c]
        o = _rms(o, gain) * go_ref[rows, :].astype(f32)
        o_ref[rows, :] = o.astype(bf16)
    st_ref[...] = st


def _gla(proj, small, w_gate, b_gate, gla_norm, bsz, seq):
    n_tok = bsz * seq
    tb = min(TB_GLA, seq)
    n_t = seq // tb

    def row(b, h, t):
        return b * n_t + t

    return pl.pallas_call(
        _gla_kernel,
        grid=(bsz, GLA_HEADS, n_t),
        in_specs=[
            pl.BlockSpec((tb, GLA_DK), lambda b, h, t: (row(b, h, t), OFF_Q // GLA_DK + h)),
            pl.BlockSpec((tb, GLA_DK), lambda b, h, t: (row(b, h, t), OFF_K // GLA_DK + h)),
            pl.BlockSpec((tb, GLA_DV), lambda b, h, t: (row(b, h, t), OFF_V // GLA_DV + h)),
            pl.BlockSpec((tb, GLA_DV), lambda b, h, t: (row(b, h, t), OFF_GO // GLA_DV + h)),
            pl.BlockSpec((tb, LANES), lambda b, h, t: (row(b, h, t), 0)),
            pl.BlockSpec((GLA_GATE_RANK, GLA_DK), lambda b, h, t: (0, h)),
            pl.BlockSpec((1, GLA_DK), lambda b, h, t: (0, h)),
            pl.BlockSpec((1, GLA_DV), lambda b, h, t: (0, 0)),
        ],
        out_specs=pl.BlockSpec((tb, GLA_DV), lambda b, h, t: (row(b, h, t), h)),
        out_shape=jax.ShapeDtypeStruct((n_tok, GLA_VAL), bf16),
        scratch_shapes=[pltpu.VMEM((GLA_DV, GLA_DK), f32),
                        pltpu.VMEM((tb, GLA_DK), f32)],
        compiler_params=pltpu.CompilerParams(
            dimension_semantics=("arbitrary", "arbitrary", "arbitrary"),
            vmem_limit_bytes=VMEM_LIMIT),
        name="gla",
    )(proj, proj, proj, proj, small, w_gate, b_gate, gla_norm)


def _ssd_kernel(z_ref, xc_ref, sm_ref, dtb_ref, arow_ref, dsk_ref, nrm_ref, e_ref,
                o_ref, st_ref):
    tb = xc_ref.shape[0]
    L = SSM_CHUNK

    @pl.when(pl.program_id(1) == 0)
    def _():
        st_ref[...] = jnp.zeros_like(st_ref)

    tri, causal = _tri(L, bf16)
    lane = lax.broadcasted_iota(jnp.int32, (L, LANES), 1)
    left = lane < SSM_HEADDIM

    n_chunks = tb // L
    chunks = [slice(c * L, (c + 1) * L) for c in range(n_chunks)]
    groups = [slice(g * SSM_GROUP_WIDTH, (g + 1) * SSM_GROUP_WIDTH) for g in range(SSM_GROUPS)]
    CD = 2 * SUBLANES

    dts = [jax.nn.softplus(sm_ref[rows, :] + dtb_ref[...]) for rows in chunks]
    parts = []
    for dt in dts:
        parts.extend(_split3(dt * arow_ref[...]))
    cs_all = _dot(tri, jnp.concatenate(parts, axis=1))
    a_cs, a_cs_t, lhs_main, lhs_cd = [], [], [], []
    for c, dt in enumerate(dts):
        o3 = 3 * LANES * c
        acs = (cs_all[:, o3:o3 + LANES] + cs_all[:, o3 + LANES:o3 + 2 * LANES]) \
            + cs_all[:, o3 + 2 * LANES:o3 + 3 * LANES]
        a_last = acs[L - 1:L, :]
        a_cs.append(acs)
        a_cs_t.append(acs.T)
        lhs_main += [dt.astype(bf16), (dt * jnp.exp(a_last - acs)).astype(bf16),
                     jnp.exp(acs).astype(bf16)]
        lhs_cd += list(_split3(jnp.broadcast_to(jnp.exp(a_last), (CD, LANES))))
    lhs = jnp.concatenate(lhs_main + lhs_cd, axis=0)

    exp_g = [_dot(lhs, e_ref[g]) for g in range(SSM_GROUPS)]

    def expanded(g, c, k):
        r0 = (3 * c + k) * L
        return exp_g[g][r0:r0 + L, :]

    def chunk_decay(g, c):
        r0 = 3 * L * n_chunks + 3 * CD * c
        r = exp_g[g]
        return (r[r0:r0 + 1, :] + r[r0 + CD:r0 + CD + 1, :]) + r[r0 + 2 * CD:r0 + 2 * CD + 1, :]

    pairs = [(c, g) for c in range(n_chunks) for g in range(SSM_GROUPS)]
    bm, cm, xg, scores, xdt, upd = {}, {}, {}, {}, {}, {}
    for c, g in pairs:
        rows = chunks[c]
        bm[c, g] = xc_ref[rows, SSM_INNER + g * SSM_STATE:SSM_INNER + (g + 1) * SSM_STATE]
        cm[c, g] = xc_ref[rows, SSM_INNER + SSM_BC + g * SSM_STATE:
                          SSM_INNER + SSM_BC + (g + 1) * SSM_STATE]
        xg[c, g] = xc_ref[rows, groups[g]].astype(f32)
        scores[c, g] = _dot_nt(cm[c, g], bm[c, g])
        xdt[c, g] = xg[c, g] * expanded(g, c, 0)
        upd[c, g] = _dot_tn(bm[c, g], (xg[c, g] * expanded(g, c, 1)).astype(bf16))

    y_diag = {}
    for c, g in pairs:
        tiles = []
        for qd in range(SSM_HPG // 2):
            ms = []
            for j in (2 * qd, 2 * qd + 1):
                ln = SM_DT + g * SSM_HPG + j
                col = jnp.broadcast_to(a_cs[c][:, ln:ln + 1], (L, L))
                rw = jnp.broadcast_to(a_cs_t[c][ln:ln + 1, :], (L, L))
                ldec = jnp.exp(jnp.where(causal, col - rw, -jnp.inf))
                ms.append((scores[c, g] * ldec).astype(bf16))
            xt = xdt[c, g][:, qd * LANES:(qd + 1) * LANES]
            xcat = jnp.concatenate([jnp.where(left, xt, 0.0), jnp.where(left, 0.0, xt)],
                                   axis=0).astype(bf16)
            tiles.append(_dot(jnp.concatenate(ms, axis=1), xcat))
        y_diag[c, g] = jnp.concatenate(tiles, axis=1)

    for g in range(SSM_GROUPS):
        st = st_ref[g]
        for c in range(n_chunks):
            rows = chunks[c]
            y = (y_diag[c, g] + _dot(cm[c, g], st.astype(bf16)) * expanded(g, c, 2)
                 + xg[c, g] * dsk_ref[:, groups[g]])
            st = st * chunk_decay(g, c) + upd[c, g]
            y = y * z_ref[rows, groups[g]].astype(f32)
            y = _rms(y, nrm_ref[:, groups[g]])
            o_ref[rows, groups[g]] = y.astype(bf16)
        st_ref[g] = st


def _ssd(proj, xbc, small, dtb_row, a_row, dsk_row, nrm, e_mat, bsz, seq):
    n_tok = bsz * seq
    tb = min(TB_SSD, seq)
    n_t = seq // tb

    def row(b, t):
        return b * n_t + t

    const2 = lambda b, t: (0, 0)
    return pl.pallas_call(
        _ssd_kernel,
        grid=(bsz, n_t),
        in_specs=[
            pl.BlockSpec((tb, SSM_INNER), lambda b, t: (row(b, t), OFF_Z // SSM_INNER)),
            pl.BlockSpec((tb, SSM_XBC), lambda b, t: (row(b, t), 0)),
            pl.BlockSpec((tb, LANES), lambda b, t: (row(b, t), 0)),
            pl.BlockSpec((1, LANES), const2),
            pl.BlockSpec((1, LANES), const2),
            pl.BlockSpec((1, SSM_INNER), const2),
            pl.BlockSpec((1, SSM_INNER), const2),
            pl.BlockSpec((SSM_GROUPS, LANES, SSM_GROUP_WIDTH), lambda b, t: (0, 0, 0)),
        ],
        out_specs=pl.BlockSpec((tb, SSM_INNER), lambda b, t: (row(b, t), 0)),
        out_shape=jax.ShapeDtypeStruct((n_tok, SSM_INNER), bf16),
        scratch_shapes=[pltpu.VMEM((SSM_GROUPS, SSM_STATE, SSM_GROUP_WIDTH), f32)],
        compiler_params=pltpu.CompilerParams(
            dimension_semantics=("arbitrary", "arbitrary"),
            vmem_limit_bytes=VMEM_LIMIT),
        name="ssd",
    )(proj, xbc, small, dtb_row, a_row, dsk_row, nrm, e_mat)


def _merge_kernel(oa_ref, ob_ref, ga_ref, gb_ref, x_ref, wa_ref, wb_ref, wo_ref, o_ref):
    ma = _dot(oa_ref[...], wa_ref[...])
    mb = _dot(ob_ref[...], wb_ref[...])
    merged = (jax.nn.sigmoid(ga_ref[...].astype(f32)) * ma
              + jax.nn.sigmoid(gb_ref[...].astype(f32)) * mb)
    o_ref[...] = x_ref[...] + _dot(merged.astype(bf16), wo_ref[...])


def _resident(shape):
    nd = len(shape)
    return pl.BlockSpec(shape, lambda *_: (0,) * nd, pipeline_mode=pl.Buffered(1))


def _merge(o_a, o_b, proj, x2, wa, wb, wo):
    n_tok = x2.shape[0]
    tm = min(TM_MERGE, n_tok)
    return pl.pallas_call(
        _merge_kernel,
        grid=(n_tok // tm,),
        in_specs=[
            pl.BlockSpec((tm, GLA_VAL), lambda i: (i, 0)),
            pl.BlockSpec((tm, SSM_INNER), lambda i: (i, 0)),
            pl.BlockSpec((tm, D_MODEL), lambda i: (i, OFF_GA // D_MODEL)),
            pl.BlockSpec((tm, D_MODEL), lambda i: (i, OFF_GB // D_MODEL)),
            pl.BlockSpec((tm, D_MODEL), lambda i: (i, 0)),
            _resident((GLA_VAL, D_MODEL)),
            _resident((SSM_INNER, D_MODEL)),
            _resident((D_MODEL, D_MODEL)),
        ],
        out_specs=pl.BlockSpec((tm, D_MODEL), lambda i: (i, 0)),
        out_shape=jax.ShapeDtypeStruct((n_tok, D_MODEL), f32),
        compiler_params=pltpu.CompilerParams(
            dimension_semantics=("arbitrary",),
            vmem_limit_bytes=VMEM_LIMIT),
        name="merge",
    )(o_a, o_b, proj, proj, x2, wa, wb, wo)


def _ffn_kernel(x_ref, p_ref, fn_ref, wup_ref, cw_ref, cb_ref, wdn_ref, pn_ref,
                wpg_ref, wpp_ref, fin_ref, o_ref, g_ref, halo_ref, ae_ref):
    tm = x_ref.shape[0]
    H = SUBLANES

    @pl.when(pl.program_id(1) == 0)
    def _():
        halo_ref[...] = jnp.zeros_like(halo_ref)

    x1 = x_ref[...]
    h = _rms(x1, fn_ref[...]).astype(bf16)
    for c in range(FFN_HIDDEN // FFN_CB):
        cols = slice(c * FFN_CB, (c + 1) * FFN_CB)
        act = _dot(h, wup_ref[:, cols])
        lin = _dot(h, wup_ref[:, FFN_HIDDEN + c * FFN_CB:FFN_HIDDEN + (c + 1) * FFN_CB])
        ae_ref[0:H, :] = halo_ref[:, cols]
        ae_ref[H:H + tm, :] = act
        halo_ref[:, cols] = act[tm - H:tm, :]
        conv = cb_ref[:, cols]
        for k in range(FFN_CONV):
            conv = conv + cw_ref[k:k + 1, cols] * ae_ref[pl.ds(H - (FFN_CONV - 1) + k, tm), :]
        g_ref[:, cols] = (jax.nn.gelu(conv) * lin).astype(bf16)

    x2 = x1 + _dot(g_ref[...], wdn_ref[...])
    gate = jax.nn.sigmoid(_dot(_rms(x2, pn_ref[...]).astype(bf16), wpg_ref[...]))
    x3 = x2 + gate * _dot(p_ref[...].astype(bf16), wpp_ref[...])
    o_ref[...] = _rms(x3, fin_ref[...])


def _ffn(x1, p2, ffn_norm, wup, cw, cb, wdn, ple_norm, wpg, wpp, final_norm, bsz, seq):
    n_tok = bsz * seq
    tm = min(TM_FFN, seq)
    n_t = seq // tm
    rowmap = lambda b, t: (b * n_t + t, 0)
    return pl.pallas_call(
        _ffn_kernel,
        grid=(bsz, n_t),
        in_specs=[
            pl.BlockSpec((tm, D_MODEL), rowmap),
            pl.BlockSpec((tm, PLE_DIM), rowmap),
            _resident((1, D_MODEL)),
            _resident((D_MODEL, 2 * FFN_HIDDEN)),
            _resident((FFN_CONV, FFN_HIDDEN)),
            _resident((1, FFN_HIDDEN)),
            _resident((FFN_HIDDEN, D_MODEL)),
            _resident((1, D_MODEL)),
            _resident((D_MODEL, D_MODEL)),
            _resident((PLE_DIM, D_MODEL)),
            _resident((1, D_MODEL)),
        ],
        out_specs=pl.BlockSpec((tm, D_MODEL), rowmap),
        out_shape=jax.ShapeDtypeStruct((n_tok, D_MODEL), f32),
        scratch_shapes=[
            pltpu.VMEM((tm, FFN_HIDDEN), bf16),
            pltpu.VMEM((SUBLANES, FFN_HIDDEN), f32),
            pltpu.VMEM((tm + SUBLANES, FFN_CB), f32),
        ],
        compiler_params=pltpu.CompilerParams(
            dimension_semantics=("arbitrary", "arbitrary"),
            vmem_limit_bytes=VMEM_LIMIT),
        name="ffn",
    )(x1, p2, ffn_norm, wup, cw, cb, wdn, ple_norm, wpg, wpp, final_norm)


_IN_SIZES = (GLA_KEY, GLA_KEY, GLA_VAL, GLA_VAL, GLA_GATE_RANK, SSM_INNER, SSM_XBC, SSM_HEADS,
             D_MODEL, D_MODEL)
_IN_OFFS = tuple(sum(_IN_SIZES[:n]) for n in range(len(_IN_SIZES) + 1))
_IN_WIDTH = _IN_OFFS[-1]
_REPACK_SRC = (tuple(_IN_OFFS[6] + TN_PROJ * n for n in range(SSM_XBC // TN_PROJ))
               + tuple(_IN_OFFS[5] + TN_PROJ * n for n in range(SSM_INNER // TN_PROJ))
               + (_IN_OFFS[0],)
               + tuple(_IN_OFFS[2] + TN_PROJ * n for n in range(GLA_VAL // TN_PROJ))
               + tuple(_IN_OFFS[3] + TN_PROJ * n for n in range(GLA_VAL // TN_PROJ))
               + (_IN_OFFS[8], _IN_OFFS[9]))
assert len(_REPACK_SRC) == _N_BLOCKS and 2 * GLA_KEY == TN_PROJ
assert all(s % (2 * SUBLANES) == 0 for s in _REPACK_SRC)


def _repack_kernel(w_ref, o_ref):
    o_ref[...] = w_ref[...].astype(bf16)


def _repack(w_t):
    def src_row(b):
        tile = functools.reduce(lambda acc, ns: jnp.where(b == ns[0], ns[1] // (2 * SUBLANES), acc),
                                list(enumerate(_REPACK_SRC)), 0)
        return (tile * (2 * SUBLANES), 0)

    return pl.pallas_call(
        _repack_kernel,
        grid=(_N_BLOCKS,),
        in_specs=[pl.BlockSpec((pl.Element(TN_PROJ), pl.Element(D_MODEL)), src_row)],
        out_specs=pl.BlockSpec((TN_PROJ, D_MODEL), lambda b: (b, 0)),
        out_shape=jax.ShapeDtypeStruct((_N_BLOCKS * TN_PROJ, D_MODEL), bf16),
        compiler_params=pltpu.CompilerParams(dimension_semantics=("arbitrary",)),
        name="repack",
    )(w_t)


def _lane_row(vals, offset):
    return jnp.zeros((1, LANES), f32).at[0, offset:offset + vals.shape[0]].set(vals.astype(f32))


def _expand_mats():
    src = jnp.arange(LANES)[None, :, None]
    dst = jnp.arange(SSM_GROUP_WIDTH)[None, None, :]
    g = jnp.arange(SSM_GROUPS)[:, None, None]
    return (src == SM_DT + g * SSM_HPG + dst // SSM_HEADDIM).astype(bf16)


def _layer(x2, p2, bsz, seq, mixer_norm, w_in, w_gla_gate, b_gla_gate, gla_norm, w_ssm_conv,
           b_ssm_conv, dt_bias, a_log, d_skip, ssm_norm, w_branch_a, w_branch_b, w_out,
           ffn_norm, w_ffn_up, w_ffn_conv, b_ffn_conv, w_ffn_down, ple_norm, w_ple_gate,
           w_ple_proj, out_norm):
    assert w_in.shape == (D_MODEL, _IN_WIDTH)
    w_t = w_in.T
    w_all = _repack(w_t)
    w_small = jnp.zeros((LANES, D_MODEL), f32)
    w_small = w_small.at[SM_ALR:SM_ALR + GLA_GATE_RANK].set(w_t[_IN_OFFS[4]:_IN_OFFS[5]])
    w_small = w_small.at[SM_DT:SM_DT + SSM_HEADS].set(w_t[_IN_OFFS[7]:_IN_OFFS[8]]).astype(bf16)

    proj, xbc, small = _inproj(x2, mixer_norm[None, :], w_all, w_small, w_ssm_conv,
                               b_ssm_conv[None, :], seq)

    o_a = _gla(proj, small, w_gla_gate, b_gla_gate[None, :], gla_norm[None, :], bsz, seq)

    o_b = _ssd(proj, xbc, small,
               _lane_row(dt_bias, SM_DT), _lane_row(-jnp.exp(a_log.astype(f32)), SM_DT),
               jnp.repeat(d_skip.astype(f32), SSM_HEADDIM)[None, :], ssm_norm[None, :],
               _expand_mats(), bsz, seq)

    x1 = _merge(o_a, o_b, proj, x2, w_branch_a.astype(bf16), w_branch_b.astype(bf16),
                w_out.astype(bf16))

    return _ffn(x1, p2, ffn_norm[None, :], w_ffn_up.astype(bf16), w_ffn_conv, b_ffn_conv[None, :],
                w_ffn_down.astype(bf16), ple_norm[None, :], w_ple_gate.astype(bf16),
                w_ple_proj.astype(bf16), out_norm[None, :], bsz, seq)


def kernel(x, p, mixer_norm, w_in, w_gla_gate, b_gla_gate, gla_norm, w_ssm_conv, b_ssm_conv,
           dt_bias, a_log, d_skip, ssm_norm, w_branch_a, w_branch_b, w_out, ffn_norm, w_ffn_up,
           w_ffn_conv, b_ffn_conv, w_ffn_down, ple_norm, w_ple_gate, w_ple_proj, final_norm):
    bsz, seq, _ = x.shape
    depth = w_in.shape[0]
    assert depth == 1, "the fused final RMSNorm assumes a single layer"
    x2 = x.reshape(bsz * seq, D_MODEL)
    out = _layer(x2, p[0].reshape(bsz * seq, PLE_DIM), bsz, seq, mixer_norm[0], w_in[0],
                 w_gla_gate[0], b_gla_gate[0], gla_norm[0], w_ssm_conv[0], b_ssm_conv[0],
                 dt_bias[0], a_log[0], d_skip[0], ssm_norm[0], w_branch_a[0], w_branch_b[0],
                 w_out[0], ffn_norm[0], w_ffn_up[0], w_ffn_conv[0], b_ffn_conv[0], w_ffn_down[0],
                 ple_norm[0], w_ple_gate[0], w_ple_proj[0], final_norm)
    return out.reshape(bsz, seq, D_MODEL)
```

```python
import functools

import jax
import jax.numpy as jnp
from jax import lax
from jax.experimental import pallas as pl
from jax.experimental.pallas import tpu as pltpu

f32 = jnp.float32
bf16 = jnp.bfloat16

D_MODEL = 1024
GLA_HEADS = 4
GLA_DK = 128
GLA_DV = 256
GLA_KEY = GLA_HEADS * GLA_DK
GLA_VAL = GLA_HEADS * GLA_DV
GLA_GATE_RANK = 16
GLA_GATE_NORM = 16.0
GLA_CHUNK = 64
SSM_INNER = 2 * D_MODEL
SSM_HEADDIM = 64
SSM_HEADS = SSM_INNER // SSM_HEADDIM
SSM_GROUPS = 4
SSM_HPG = SSM_HEADS // SSM_GROUPS
SSM_STATE = 128
SSM_CONV = 4
SSM_CHUNK = 128
SSM_BC = SSM_GROUPS * SSM_STATE
SSM_XBC = SSM_INNER + 2 * SSM_BC
SSM_GROUP_WIDTH = SSM_INNER // SSM_GROUPS
FFN_HIDDEN = 2816
FFN_CONV = 3
PLE_DIM = 256
EPS = 1e-6

LANES = 128
SUBLANES = 8
VMEM_LIMIT = 56 * 1024 * 1024

OFF_Z = 0
OFF_Q = OFF_Z + SSM_INNER
OFF_K = OFF_Q + GLA_KEY
OFF_V = OFF_K + GLA_KEY
OFF_GO = OFF_V + GLA_VAL
OFF_GA = OFF_GO + GLA_VAL
OFF_GB = OFF_GA + D_MODEL
MAIN_WIDTH = OFF_GB + D_MODEL
assert OFF_V % GLA_DV == 0 and OFF_GA % D_MODEL == 0
SM_ALR = 0
SM_DT = 32

TM_PROJ = 2048
TN_PROJ = 1024
TB_GLA = 2048
TB_SSD = 512
TM_MERGE = 1024
TM_FFN = 512
FFN_CB = 256


def _split3(x):
    hi = x.astype(bf16)
    r1 = x - hi.astype(f32)
    mid = r1.astype(bf16)
    lo = (r1 - mid.astype(f32)).astype(bf16)
    return hi, mid, lo


def _dot(a, b):
    return jnp.dot(a, b, preferred_element_type=f32)


def _dot_nt(a, b):
    return lax.dot_general(a, b, (((1,), (1,)), ((), ())), preferred_element_type=f32)


def _dot_tn(a, b):
    return lax.dot_general(a, b, (((0,), (0,)), ((), ())), preferred_element_type=f32)


def _rms(x, gain):
    return x * lax.rsqrt(jnp.mean(x * x, axis=-1, keepdims=True) + EPS) * gain


def _silu_of_half(h):
    return h + h * jnp.tanh(h)


def _tri(n, dtype):
    r = lax.broadcasted_iota(jnp.int32, (n, n), 0)
    c = lax.broadcasted_iota(jnp.int32, (n, n), 1)
    return (r >= c).astype(dtype), r >= c


PROJ_HALO = 16
PROJ_CS = 256
PROJ_RC = 512
_CONV_BLOCKS = SSM_XBC // TN_PROJ
_N_BLOCKS = _CONV_BLOCKS + MAIN_WIDTH // TN_PROJ
_SILU_BLOCKS = tuple(_CONV_BLOCKS + b for b in
                     tuple(range(OFF_Z // TN_PROJ, (OFF_Z + SSM_INNER) // TN_PROJ))
                     + tuple(range(OFF_GO // TN_PROJ, (OFF_GO + GLA_VAL) // TN_PROJ)))
assert OFF_Z % TN_PROJ == 0 and OFF_GO % TN_PROJ == 0
assert SSM_XBC % TN_PROJ == 0 and SSM_INNER % TN_PROJ == 0 and GLA_VAL % TN_PROJ == 0
assert SSM_CONV == 4


def _inproj_kernel(x_ref, xp_ref, g_ref, w_ref, ws_ref, cw_ref, cb_ref, o_ref, oc_ref, os_ref,
                   h_ref, *, tiles_per_seq):
    i = pl.program_id(0)
    j = pl.program_id(1)
    tm = x_ref.shape[0]

    @pl.when(j == 0)
    def _():
        hb = _rms(x_ref[...], g_ref[...]).astype(bf16)
        h_ref[PROJ_HALO:, :] = hb
        os_ref[...] = _dot_nt(hb, ws_ref[...])
        hp = _rms(xp_ref[...], g_ref[...])
        first = (i % tiles_per_seq) == 0
        h_ref[0:PROJ_HALO, :] = jnp.where(first, 0.0, hp).astype(bf16)

    @pl.when(j < _CONV_BLOCKS)
    def _():
        rows = min(PROJ_RC, tm)
        n_rc = tm // rows
        pieces = [(cs, rc) for cs in range(TN_PROJ // PROJ_CS) for rc in range(n_rc)]

        def raw(cs, rc):
            lo = 0 if rc == 0 else PROJ_HALO + rc * rows
            return _dot_nt(h_ref[lo:PROJ_HALO + (rc + 1) * rows, :],
                           w_ref[cs * PROJ_CS:(cs + 1) * PROJ_CS, :])

        def conv(cs, rc, ext):
            cols = slice(cs * PROJ_CS, (cs + 1) * PROJ_CS)
            taps = 0.5 * cw_ref[:, cols]
            prev = pltpu.roll(ext, 1, 0)
            older = pltpu.roll(taps[1:2, :] * ext + taps[0:1, :] * prev, 2, 0)
            acc = (0.5 * cb_ref[:, cols] + taps[3:4, :] * ext[SUBLANES:, :]
                   + taps[2:3, :] * prev[SUBLANES:, :] + older[SUBLANES:, :])
            oc_ref[rc * rows:(rc + 1) * rows, cols] = _silu_of_half(acc).astype(bf16)

        r = raw(*pieces[0])
        tail = None
        for n, (cs, rc) in enumerate(pieces):
            r_next = raw(*pieces[n + 1]) if n + 1 < len(pieces) else None
            ext = r[PROJ_HALO - SUBLANES:, :] if rc == 0 else jnp.concatenate([tail, r], axis=0)
            tail = r[r.shape[0] - SUBLANES:, :]
            conv(cs, rc, ext)
            r = r_next

    is_silu = functools.reduce(lambda a, b: a | b, [j == b for b in _SILU_BLOCKS])

    @pl.when(is_silu)
    def _():
        half = _dot_nt(h_ref[PROJ_HALO:, :], w_ref[...])
        o_ref[...] = _silu_of_half(half).astype(bf16)

    @pl.when((j >= _CONV_BLOCKS) & jnp.logical_not(is_silu))
    def _():
        o_ref[...] = _dot_nt(h_ref[PROJ_HALO:, :], w_ref[...]).astype(bf16)


def _inproj(x2, gain, w_all, w_small, cw, cb, seq):
    n_tok = x2.shape[0]
    tm = min(TM_PROJ, seq)
    grid = (n_tok // tm, _N_BLOCKS)
    halo_blocks = tm // PROJ_HALO

    def conv_block(i, j):
        return (0, jnp.minimum(j, _CONV_BLOCKS - 1))

    return pl.pallas_call(
        functools.partial(_inproj_kernel, tiles_per_seq=seq // tm),
        grid=grid,
        in_specs=[
            pl.BlockSpec((tm, D_MODEL), lambda i, j: (i, 0)),
            pl.BlockSpec((PROJ_HALO, D_MODEL), lambda i, j: (jnp.maximum(i * halo_blocks - 1, 0), 0)),
            pl.BlockSpec((1, D_MODEL), lambda i, j: (0, 0)),
            pl.BlockSpec((TN_PROJ, D_MODEL), lambda i, j: (j, 0)),
            pl.BlockSpec((LANES, D_MODEL), lambda i, j: (0, 0)),
            pl.BlockSpec((SSM_CONV, TN_PROJ), conv_block),
            pl.BlockSpec((1, TN_PROJ), conv_block),
        ],
        out_specs=[
            pl.BlockSpec((tm, TN_PROJ), lambda i, j: (i, jnp.maximum(j - _CONV_BLOCKS, 0))),
            pl.BlockSpec((tm, TN_PROJ), lambda i, j: (i, jnp.minimum(j, _CONV_BLOCKS - 1))),
            pl.BlockSpec((tm, LANES), lambda i, j: (i, 0)),
        ],
        out_shape=[
            jax.ShapeDtypeStruct((n_tok, MAIN_WIDTH), bf16),
            jax.ShapeDtypeStruct((n_tok, SSM_XBC), bf16),
            jax.ShapeDtypeStruct((n_tok, LANES), f32),
        ],
        scratch_shapes=[pltpu.VMEM((PROJ_HALO + tm, D_MODEL), bf16)],
        compiler_params=pltpu.CompilerParams(
            dimension_semantics=("arbitrary", "arbitrary"),
            vmem_limit_bytes=VMEM_LIMIT),
        name="inproj",
    )(x2, x2, gain, w_all, w_small, cw, cb)


def _gla_kernel(q_ref, k_ref, v_ref, go_ref, sm_ref, wg_ref, bg_ref, gn_ref,
                o_ref, st_ref, gl_ref):
    C = GLA_CHUNK

    @pl.when(pl.program_id(2) == 0)
    def _():
        st_ref[...] = jnp.zeros_like(st_ref)

    a_lr = sm_ref[:, SM_ALR:SM_ALR + GLA_GATE_RANK].astype(bf16)
    logit = _dot(a_lr, wg_ref[...].astype(bf16)) + bg_ref[...]
    gl_ref[...] = (jnp.minimum(logit, 0.0)
                   - jnp.log1p(jnp.exp(-jnp.abs(logit)))) * (1.0 / GLA_GATE_NORM)

    tri, causal = _tri(C, bf16)
    gain = gn_ref[...]
    n_chunks = q_ref.shape[0] // C

    chunks = [slice(c * C, (c + 1) * C) for c in range(n_chunks)]
    b3s = []
    for rows in chunks:
        hi, mid, lo = _split3(gl_ref[rows, :])
        b3s.append(_dot(tri, jnp.concatenate([hi, mid, lo], axis=1)))
    q_tb, k_tb, k_db, decay = [], [], [], []
    for rows, b3 in zip(chunks, b3s):
        b = (b3[:, 0:GLA_DK] + b3[:, GLA_DK:2 * GLA_DK]) + b3[:, 2 * GLA_DK:3 * GLA_DK]
        b_last = b[C - 1:C, :]
        kf = k_ref[rows, :].astype(f32)
        q_tb.append(((q_ref[rows, :].astype(f32) * (GLA_DK ** -0.5)) * jnp.exp(b)).astype(bf16))
        k_tb.append((kf * jnp.exp(-b)).astype(bf16))
        k_db.append((kf * jnp.exp(b_last - b)).astype(bf16))
        decay.append(jnp.exp(b_last))
    att = [_dot_nt(q_tb[c], k_tb[c]) for c in range(n_chunks)]
    upd = [_dot_tn(v_ref[chunks[c], :], k_db[c]) for c in range(n_chunks)]
    o_intra = [_dot(jnp.where(causal, att[c], 0.0).astype(bf16), v_ref[chunks[c], :])
               for c in range(n_chunks)]
    st = st_ref[...]
    for c, rows in enumerate(chunks):
        o = o_intra[c] + _dot_nt(q_tb[c], st.astype(bf16))
        st = decay[c] * st + upd[c]
        o = _rms(o, gain) * go_ref[rows, :].astype(f32)
        o_ref[rows, :] = o.astype(bf16)
    st_ref[...] = st


def _gla(proj, small, w_gate, b_gate, gla_norm, bsz, seq):
    n_tok = bsz * seq
    tb = min(TB_GLA, seq)
    n_t = seq // tb

    def row(b, h, t):
        return b * n_t + t

    return pl.pallas_call(
        _gla_kernel,
        grid=(bsz, GLA_HEADS, n_t),
        in_specs=[
            pl.BlockSpec((tb, GLA_DK), lambda b, h, t: (row(b, h, t), OFF_Q // GLA_DK + h)),
            pl.BlockSpec((tb, GLA_DK), lambda b, h, t: (row(b, h, t), OFF_K // GLA_DK + h)),
            pl.BlockSpec((tb, GLA_DV), lambda b, h, t: (row(b, h, t), OFF_V // GLA_DV + h)),
            pl.BlockSpec((tb, GLA_DV), lambda b, h, t: (row(b, h, t), OFF_GO // GLA_DV + h)),
            pl.BlockSpec((tb, LANES), lambda b, h, t: (row(b, h, t), 0)),
            pl.BlockSpec((GLA_GATE_RANK, GLA_DK), lambda b, h, t: (0, h)),
            pl.BlockSpec((1, GLA_DK), lambda b, h, t: (0, h)),
            pl.BlockSpec((1, GLA_DV), lambda b, h, t: (0, 0)),
        ],
        out_specs=pl.BlockSpec((tb, GLA_DV), lambda b, h, t: (row(b, h, t), h)),
        out_shape=jax.ShapeDtypeStruct((n_tok, GLA_VAL), bf16),
        scratch_shapes=[pltpu.VMEM((GLA_DV, GLA_DK), f32),
                        pltpu.VMEM((tb, GLA_DK), f32)],
        compiler_params=pltpu.CompilerParams(
            dimension_semantics=("arbitrary", "arbitrary", "arbitrary"),
            vmem_limit_bytes=VMEM_LIMIT),
        name="gla",
    )(proj, proj, proj, proj, small, w_gate, b_gate, gla_norm)


def _ssd_kernel(z_ref, xc_ref, sm_ref, dtb_ref, arow_ref, dsk_ref, nrm_ref, e_ref,
                o_ref, st_ref):
    tb = xc_ref.shape[0]
    L = SSM_CHUNK

    @pl.when(pl.program_id(1) == 0)
    def _():
        st_ref[...] = jnp.zeros_like(st_ref)

    tri, causal = _tri(L, bf16)
    lane = lax.broadcasted_iota(jnp.int32, (L, LANES), 1)
    left = lane < SSM_HEADDIM

    n_chunks = tb // L
    chunks = [slice(c * L, (c + 1) * L) for c in range(n_chunks)]
    groups = [slice(g * SSM_GROUP_WIDTH, (g + 1) * SSM_GROUP_WIDTH) for g in range(SSM_GROUPS)]
    CD = 2 * SUBLANES

    dts = [jax.nn.softplus(sm_ref[rows, :] + dtb_ref[...]) for rows in chunks]
    parts = []
    for dt in dts:
        parts.extend(_split3(dt * arow_ref[...]))
    cs_all = _dot(tri, jnp.concatenate(parts, axis=1))
    a_cs, a_cs_t, lhs_main, lhs_cd = [], [], [], []
    for c, dt in enumerate(dts):
        o3 = 3 * LANES * c
        acs = (cs_all[:, o3:o3 + LANES] + cs_all[:, o3 + LANES:o3 + 2 * LANES]) \
            + cs_all[:, o3 + 2 * LANES:o3 + 3 * LANES]
        a_last = acs[L - 1:L, :]
        a_cs.append(acs)
        a_cs_t.append((acs - jnp.log(dt)).T)
        lhs_main += [(dt * jnp.exp(a_last - acs)).astype(bf16), jnp.exp(acs).astype(bf16)]
        lhs_cd += list(_split3(jnp.broadcast_to(jnp.exp(a_last), (CD, LANES))))
    lhs = jnp.concatenate(lhs_main + lhs_cd, axis=0)

    exp_g = [_dot(lhs, e_ref[g]) for g in range(SSM_GROUPS)]

    def expanded(g, c, k):
        r0 = (2 * c + k) * L
        return exp_g[g][r0:r0 + L, :]

    def chunk_decay(g, c):
        r0 = 2 * L * n_chunks + 3 * CD * c
        r = exp_g[g]
        return (r[r0:r0 + 1, :] + r[r0 + CD:r0 + CD + 1, :]) + r[r0 + 2 * CD:r0 + 2 * CD + 1, :]

    pairs = [(c, g) for c in range(n_chunks) for g in range(SSM_GROUPS)]
    bm, cm, xg, scores, upd = {}, {}, {}, {}, {}
    for c, g in pairs:
        rows = chunks[c]
        bm[c, g] = xc_ref[rows, SSM_INNER + g * SSM_STATE:SSM_INNER + (g + 1) * SSM_STATE]
        cm[c, g] = xc_ref[rows, SSM_INNER + SSM_BC + g * SSM_STATE:
                          SSM_INNER + SSM_BC + (g + 1) * SSM_STATE]
        xg[c, g] = xc_ref[rows, groups[g]].astype(f32)
        scores[c, g] = _dot_nt(cm[c, g], bm[c, g])
        upd[c, g] = _dot_tn(bm[c, g], (xg[c, g] * expanded(g, c, 0)).astype(bf16))

    y_diag = {}
    for c, g in pairs:
        tiles = []
        for qd in range(SSM_HPG // 2):
            xt = xc_ref[chunks[c], g * SSM_GROUP_WIDTH + qd * LANES:
                        g * SSM_GROUP_WIDTH + (qd + 1) * LANES]
            halves = []
            for j in (2 * qd, 2 * qd + 1):
                ln = SM_DT + g * SSM_HPG + j
                col = jnp.broadcast_to(a_cs[c][:, ln:ln + 1], (L, L))
                rw = jnp.broadcast_to(a_cs_t[c][ln:ln + 1, :], (L, L))
                ldec = jnp.exp(jnp.where(causal, col - rw, -jnp.inf))
                halves.append(_dot((scores[c, g] * ldec).astype(bf16), xt))
            tiles.append(jnp.where(left, halves[0], halves[1]))
        y_diag[c, g] = jnp.concatenate(tiles, axis=1)

    for g in range(SSM_GROUPS):
        st = st_ref[g]
        for c in range(n_chunks):
            rows = chunks[c]
            y = (y_diag[c, g] + _dot(cm[c, g], st.astype(bf16)) * expanded(g, c, 1)
                 + xg[c, g] * dsk_ref[:, groups[g]])
            st = st * chunk_decay(g, c) + upd[c, g]
            y = y * z_ref[rows, groups[g]].astype(f32)
            y = _rms(y, nrm_ref[:, groups[g]])
            o_ref[rows, groups[g]] = y.astype(bf16)
        st_ref[g] = st


def _ssd(proj, xbc, small, dtb_row, a_row, dsk_row, nrm, e_mat, bsz, seq):
    n_tok = bsz * seq
    tb = min(TB_SSD, seq)
    n_t = seq // tb

    def row(b, t):
        return b * n_t + t

    const2 = lambda b, t: (0, 0)
    return pl.pallas_call(
        _ssd_kernel,
        grid=(bsz, n_t),
        in_specs=[
            pl.BlockSpec((tb, SSM_INNER), lambda b, t: (row(b, t), OFF_Z // SSM_INNER)),
            pl.BlockSpec((tb, SSM_XBC), lambda b, t: (row(b, t), 0)),
            pl.BlockSpec((tb, LANES), lambda b, t: (row(b, t), 0)),
            pl.BlockSpec((1, LANES), const2),
            pl.BlockSpec((1, LANES), const2),
            pl.BlockSpec((1, SSM_INNER), const2),
            pl.BlockSpec((1, SSM_INNER), const2),
            pl.BlockSpec((SSM_GROUPS, LANES, SSM_GROUP_WIDTH), lambda b, t: (0, 0, 0)),
        ],
        out_specs=pl.BlockSpec((tb, SSM_INNER), lambda b, t: (row(b, t), 0)),
        out_shape=jax.ShapeDtypeStruct((n_tok, SSM_INNER), bf16),
        scratch_shapes=[pltpu.VMEM((SSM_GROUPS, SSM_STATE, SSM_GROUP_WIDTH), f32)],
        compiler_params=pltpu.CompilerParams(
            dimension_semantics=("arbitrary", "arbitrary"),
            vmem_limit_bytes=VMEM_LIMIT),
        name="ssd",
    )(proj, xbc, small, dtb_row, a_row, dsk_row, nrm, e_mat)


def _merge_kernel(oa_ref, ob_ref, ga_ref, gb_ref, x_ref, wa_ref, wb_ref, wo_ref, o_ref):
    ma = _dot(oa_ref[...], wa_ref[...])
    mb = _dot(ob_ref[...], wb_ref[...])
    merged = (jax.nn.sigmoid(ga_ref[...].astype(f32)) * ma
              + jax.nn.sigmoid(gb_ref[...].astype(f32)) * mb)
    o_ref[...] = x_ref[...] + _dot(merged.astype(bf16), wo_ref[...])


def _resident(shape):
    nd = len(shape)
    return pl.BlockSpec(shape, lambda *_: (0,) * nd, pipeline_mode=pl.Buffered(1))


def _merge(o_a, o_b, proj, x2, wa, wb, wo):
    n_tok = x2.shape[0]
    tm = min(TM_MERGE, n_tok)
    return pl.pallas_call(
        _merge_kernel,
        grid=(n_tok // tm,),
        in_specs=[
            pl.BlockSpec((tm, GLA_VAL), lambda i: (i, 0)),
            pl.BlockSpec((tm, SSM_INNER), lambda i: (i, 0)),
            pl.BlockSpec((tm, D_MODEL), lambda i: (i, OFF_GA // D_MODEL)),
            pl.BlockSpec((tm, D_MODEL), lambda i: (i, OFF_GB // D_MODEL)),
            pl.BlockSpec((tm, D_MODEL), lambda i: (i, 0)),
            _resident((GLA_VAL, D_MODEL)),
            _resident((SSM_INNER, D_MODEL)),
            _resident((D_MODEL, D_MODEL)),
        ],
        out_specs=pl.BlockSpec((tm, D_MODEL), lambda i: (i, 0)),
        out_shape=jax.ShapeDtypeStruct((n_tok, D_MODEL), f32),
        compiler_params=pltpu.CompilerParams(
            dimension_semantics=("arbitrary",),
            vmem_limit_bytes=VMEM_LIMIT),
        name="merge",
    )(o_a, o_b, proj, proj, x2, wa, wb, wo)


def _ffn_kernel(x_ref, p_ref, fn_ref, wup_ref, cw_ref, cb_ref, wdn_ref, pn_ref,
                wpg_ref, wpp_ref, fin_ref, o_ref, g_ref, halo_ref, ae_ref):
    tm = x_ref.shape[0]
    H = SUBLANES

    @pl.when(pl.program_id(1) == 0)
    def _():
        halo_ref[...] = jnp.zeros_like(halo_ref)

    x1 = x_ref[...]
    h = _rms(x1, fn_ref[...]).astype(bf16)
    for c in range(FFN_HIDDEN // FFN_CB):
        cols = slice(c * FFN_CB, (c + 1) * FFN_CB)
        act = _dot(h, wup_ref[:, cols])
        lin = _dot(h, wup_ref[:, FFN_HIDDEN + c * FFN_CB:FFN_HIDDEN + (c + 1) * FFN_CB])
        ae_ref[0:H, :] = halo_ref[:, cols]
        ae_ref[H:H + tm, :] = act
        halo_ref[:, cols] = act[tm - H:tm, :]
        conv = cb_ref[:, cols]
        for k in range(FFN_CONV):
            conv = conv + cw_ref[k:k + 1, cols] * ae_ref[pl.ds(H - (FFN_CONV - 1) + k, tm), :]
        g_ref[:, cols] = (jax.nn.gelu(conv) * lin).astype(bf16)

    x2 = x1 + _dot(g_ref[...], wdn_ref[...])
    gate = jax.nn.sigmoid(_dot(_rms(x2, pn_ref[...]).astype(bf16), wpg_ref[...]))
    x3 = x2 + gate * _dot(p_ref[...].astype(bf16), wpp_ref[...])
    o_ref[...] = _rms(x3, fin_ref[...])


def _ffn(x1, p2, ffn_norm, wup, cw, cb, wdn, ple_norm, wpg, wpp, final_norm, bsz, seq):
    n_tok = bsz * seq
    tm = min(TM_FFN, seq)
    n_t = seq // tm
    rowmap = lambda b, t: (b * n_t + t, 0)
    return pl.pallas_call(
        _ffn_kernel,
        grid=(bsz, n_t),
        in_specs=[
            pl.BlockSpec((tm, D_MODEL), rowmap),
            pl.BlockSpec((tm, PLE_DIM), rowmap),
            _resident((1, D_MODEL)),
            _resident((D_MODEL, 2 * FFN_HIDDEN)),
            _resident((FFN_CONV, FFN_HIDDEN)),
            _resident((1, FFN_HIDDEN)),
            _resident((FFN_HIDDEN, D_MODEL)),
            _resident((1, D_MODEL)),
            _resident((D_MODEL, D_MODEL)),
            _resident((PLE_DIM, D_MODEL)),
            _resident((1, D_MODEL)),
        ],
        out_specs=pl.BlockSpec((tm, D_MODEL), rowmap),
        out_shape=jax.ShapeDtypeStruct((n_tok, D_MODEL), f32),
        scratch_shapes=[
            pltpu.VMEM((tm, FFN_HIDDEN), bf16),
            pltpu.VMEM((SUBLANES, FFN_HIDDEN), f32),
            pltpu.VMEM((tm + SUBLANES, FFN_CB), f32),
        ],
        compiler_params=pltpu.CompilerParams(
            dimension_semantics=("arbitrary", "arbitrary"),
            vmem_limit_bytes=VMEM_LIMIT),
        name="ffn",
    )(x1, p2, ffn_norm, wup, cw, cb, wdn, ple_norm, wpg, wpp, final_norm)


_IN_SIZES = (GLA_KEY, GLA_KEY, GLA_VAL, GLA_VAL, GLA_GATE_RANK, SSM_INNER, SSM_XBC, SSM_HEADS,
             D_MODEL, D_MODEL)
_IN_OFFS = tuple(sum(_IN_SIZES[:n]) for n in range(len(_IN_SIZES) + 1))
_IN_WIDTH = _IN_OFFS[-1]
_REPACK_SRC = (tuple(_IN_OFFS[6] + TN_PROJ * n for n in range(SSM_XBC // TN_PROJ))
               + tuple(_IN_OFFS[5] + TN_PROJ * n for n in range(SSM_INNER // TN_PROJ))
               + (_IN_OFFS[0],)
               + tuple(_IN_OFFS[2] + TN_PROJ * n for n in range(GLA_VAL // TN_PROJ))
               + tuple(_IN_OFFS[3] + TN_PROJ * n for n in range(GLA_VAL // TN_PROJ))
               + (_IN_OFFS[8], _IN_OFFS[9]))
assert len(_REPACK_SRC) == _N_BLOCKS and 2 * GLA_KEY == TN_PROJ
assert all(s % (2 * SUBLANES) == 0 for s in _REPACK_SRC)


def _repack_kernel(w_ref, o_ref):
    b = pl.program_id(0)
    halved = functools.reduce(lambda a, c: a | c, [b == n for n in _SILU_BLOCKS])
    o_ref[...] = (w_ref[...] * jnp.where(halved, 0.5, 1.0)).astype(bf16)


def _repack(w_t):
    def src_row(b):
        tile = functools.reduce(lambda acc, ns: jnp.where(b == ns[0], ns[1] // (2 * SUBLANES), acc),
                                list(enumerate(_REPACK_SRC)), 0)
        return (tile * (2 * SUBLANES), 0)

    return pl.pallas_call(
        _repack_kernel,
        grid=(_N_BLOCKS,),
        in_specs=[pl.BlockSpec((pl.Element(TN_PROJ), pl.Element(D_MODEL)), src_row)],
        out_specs=pl.BlockSpec((TN_PROJ, D_MODEL), lambda b: (b, 0)),
        out_shape=jax.ShapeDtypeStruct((_N_BLOCKS * TN_PROJ, D_MODEL), bf16),
        compiler_params=pltpu.CompilerParams(dimension_semantics=("arbitrary",)),
        name="repack",
    )(w_t)


def _lane_row(vals, offset):
    return jnp.zeros((1, LANES), f32).at[0, offset:offset + vals.shape[0]].set(vals.astype(f32))


def _expand_mats():
    src = jnp.arange(LANES)[None, :, None]
    dst = jnp.arange(SSM_GROUP_WIDTH)[None, None, :]
    g = jnp.arange(SSM_GROUPS)[:, None, None]
    return (src == SM_DT + g * SSM_HPG + dst // SSM_HEADDIM).astype(bf16)


def _layer(x2, p2, bsz, seq, mixer_norm, w_in, w_gla_gate, b_gla_gate, gla_norm, w_ssm_conv,
           b_ssm_conv, dt_bias, a_log, d_skip, ssm_norm, w_branch_a, w_branch_b, w_out,
           ffn_norm, w_ffn_up, w_ffn_conv, b_ffn_conv, w_ffn_down, ple_norm, w_ple_gate,
           w_ple_proj, out_norm):
    assert w_in.shape == (D_MODEL, _IN_WIDTH)
    w_t = w_in.T
    w_all = _repack(w_t)
    w_small = jnp.zeros((LANES, D_MODEL), f32)
    w_small = w_small.at[SM_ALR:SM_ALR + GLA_GATE_RANK].set(w_t[_IN_OFFS[4]:_IN_OFFS[5]])
    w_small = w_small.at[SM_DT:SM_DT + SSM_HEADS].set(w_t[_IN_OFFS[7]:_IN_OFFS[8]]).astype(bf16)

    proj, xbc, small = _inproj(x2, mixer_norm[None, :], w_all, w_small, w_ssm_conv,
                               b_ssm_conv[None, :], seq)

    o_a = _gla(proj, small, w_gla_gate, b_gla_gate[None, :], gla_norm[None, :], bsz, seq)

    o_b = _ssd(proj, xbc, small,
               _lane_row(dt_bias, SM_DT), _lane_row(-jnp.exp(a_log.astype(f32)), SM_DT),
               jnp.repeat(d_skip.astype(f32), SSM_HEADDIM)[None, :], ssm_norm[None, :],
               _expand_mats(), bsz, seq)

    x1 = _merge(o_a, o_b, proj, x2, w_branch_a.astype(bf16), w_branch_b.astype(bf16),
                w_out.astype(bf16))

    return _ffn(x1, p2, ffn_norm[None, :], w_ffn_up.astype(bf16), w_ffn_conv, b_ffn_conv[None, :],
                w_ffn_down.astype(bf16), ple_norm[None, :], w_ple_gate.astype(bf16),
                w_ple_proj.astype(bf16), out_norm[None, :], bsz, seq)


def kernel(x, p, mixer_norm, w_in, w_gla_gate, b_gla_gate, gla_norm, w_ssm_conv, b_ssm_conv,
           dt_bias, a_log, d_skip, ssm_norm, w_branch_a, w_branch_b, w_out, ffn_norm, w_ffn_up,
           w_ffn_conv, b_ffn_conv, w_ffn_down, ple_norm, w_ple_gate, w_ple_proj, final_norm):
    bsz, seq, _ = x.shape
    depth = w_in.shape[0]
    assert depth == 1, "the fused final RMSNorm assumes a single layer"
    x2 = x.reshape(bsz * seq, D_MODEL)
    out = _layer(x2, p[0].reshape(bsz * seq, PLE_DIM), bsz, seq, mixer_norm[0], w_in[0],
                 w_gla_gate[0], b_gla_gate[0], gla_norm[0], w_ssm_conv[0], b_ssm_conv[0],
                 dt_bias[0], a_log[0], d_skip[0], ssm_norm[0], w_branch_a[0], w_branch_b[0],
                 w_out[0], ffn_norm[0], w_ffn_up[0], w_ffn_conv[0], b_ffn_conv[0], w_ffn_down[0],
                 ple_norm[0], w_ple_gate[0], w_ple_proj[0], final_norm)
    return out.reshape(bsz, seq, D_MODEL)
```

```python
import functools

import jax
import jax.numpy as jnp
from jax import lax
from jax.experimental import pallas as pl
from jax.experimental.pallas import tpu as pltpu

f32 = jnp.float32
bf16 = jnp.bfloat16

D_MODEL = 1024
GLA_HEADS = 4
GLA_DK = 128
GLA_DV = 256
GLA_KEY = GLA_HEADS * GLA_DK
GLA_VAL = GLA_HEADS * GLA_DV
GLA_GATE_RANK = 16
GLA_GATE_NORM = 16.0
GLA_CHUNK = 64
SSM_INNER = 2 * D_MODEL
SSM_HEADDIM = 64
SSM_HEADS = SSM_INNER // SSM_HEADDIM
SSM_GROUPS = 4
SSM_HPG = SSM_HEADS // SSM_GROUPS
SSM_STATE = 128
SSM_CONV = 4
SSM_CHUNK = 128
SSM_BC = SSM_GROUPS * SSM_STATE
SSM_XBC = SSM_INNER + 2 * SSM_BC
SSM_GROUP_WIDTH = SSM_INNER // SSM_GROUPS
FFN_HIDDEN = 2816
FFN_CONV = 3
PLE_DIM = 256
EPS = 1e-6

LANES = 128
SUBLANES = 8
VMEM_LIMIT = 56 * 1024 * 1024

OFF_Z = 0
OFF_Q = OFF_Z + SSM_INNER
OFF_K = OFF_Q + GLA_KEY
OFF_V = OFF_K + GLA_KEY
OFF_GO = OFF_V + GLA_VAL
OFF_GA = OFF_GO + GLA_VAL
OFF_GB = OFF_GA + D_MODEL
MAIN_WIDTH = OFF_GB + D_MODEL
assert OFF_V % GLA_DV == 0 and OFF_GA % D_MODEL == 0
SM_ALR = 0
SM_DT = 32

TM_PROJ = 2048
TN_PROJ = 1024
TB_GLA = 2048
TB_SSD = 512
TM_MERGE = 1024
TM_FFN = 512
FFN_CB = 256


def _split3(x):
    hi = x.astype(bf16)
    r1 = x - hi.astype(f32)
    mid = r1.astype(bf16)
    lo = (r1 - mid.astype(f32)).astype(bf16)
    return hi, mid, lo


def _dot(a, b):
    return jnp.dot(a, b, preferred_element_type=f32)


def _dot_nt(a, b):
    return lax.dot_general(a, b, (((1,), (1,)), ((), ())), preferred_element_type=f32)


def _dot_tn(a, b):
    return lax.dot_general(a, b, (((0,), (0,)), ((), ())), preferred_element_type=f32)


def _rms(x, gain):
    return x * lax.rsqrt(jnp.mean(x * x, axis=-1, keepdims=True) + EPS) * gain


def _silu_of_half(h):
    return h + h * jnp.tanh(h)


def _tri(n, dtype):
    r = lax.broadcasted_iota(jnp.int32, (n, n), 0)
    c = lax.broadcasted_iota(jnp.int32, (n, n), 1)
    return (r >= c).astype(dtype), r >= c


PROJ_HALO = 16
PROJ_CS = 256
PROJ_RC = 512
_CONV_BLOCKS = SSM_XBC // TN_PROJ
_N_BLOCKS = _CONV_BLOCKS + MAIN_WIDTH // TN_PROJ
_SILU_BLOCKS = tuple(_CONV_BLOCKS + b for b in
                     tuple(range(OFF_Z // TN_PROJ, (OFF_Z + SSM_INNER) // TN_PROJ))
                     + tuple(range(OFF_GO // TN_PROJ, (OFF_GO + GLA_VAL) // TN_PROJ)))
assert OFF_Z % TN_PROJ == 0 and OFF_GO % TN_PROJ == 0
assert SSM_XBC % TN_PROJ == 0 and SSM_INNER % TN_PROJ == 0 and GLA_VAL % TN_PROJ == 0
assert SSM_CONV == 4


def _inproj_kernel(x_ref, xp_ref, g_ref, w_ref, ws_ref, cw_ref, cb_ref, o_ref, oc_ref, os_ref,
                   h_ref, *, tiles_per_seq):
    i = pl.program_id(0)
    j = pl.program_id(1)
    tm = x_ref.shape[0]

    @pl.when(j == 0)
    def _():
        hb = _rms(x_ref[...], g_ref[...]).astype(bf16)
        h_ref[PROJ_HALO:, :] = hb
        os_ref[...] = _dot_nt(hb, ws_ref[...])
        hp = _rms(xp_ref[...], g_ref[...])
        first = (i % tiles_per_seq) == 0
        h_ref[0:PROJ_HALO, :] = jnp.where(first, 0.0, hp).astype(bf16)

    @pl.when(j < _CONV_BLOCKS)
    def _():
        rows = min(PROJ_RC, tm)
        n_rc = tm // rows
        pieces = [(cs, rc) for cs in range(TN_PROJ // PROJ_CS) for rc in range(n_rc)]

        w_tiles = [w_ref[cs * PROJ_CS:(cs + 1) * PROJ_CS, :].astype(bf16)
                   for cs in range(TN_PROJ // PROJ_CS)]

        def raw(cs, rc):
            lo = 0 if rc == 0 else PROJ_HALO + rc * rows
            return _dot_nt(h_ref[lo:PROJ_HALO + (rc + 1) * rows, :], w_tiles[cs])

        def conv(cs, rc, ext):
            cols = slice(cs * PROJ_CS, (cs + 1) * PROJ_CS)
            taps = 0.5 * cw_ref[:, cols]
            prev = pltpu.roll(ext, 1, 0)
            older = pltpu.roll(taps[1:2, :] * ext + taps[0:1, :] * prev, 2, 0)
            acc = (0.5 * cb_ref[:, cols] + taps[3:4, :] * ext[SUBLANES:, :]
                   + taps[2:3, :] * prev[SUBLANES:, :] + older[SUBLANES:, :])
            oc_ref[rc * rows:(rc + 1) * rows, cols] = _silu_of_half(acc).astype(bf16)

        r = raw(*pieces[0])
        tail = None
        for n, (cs, rc) in enumerate(pieces):
            r_next = raw(*pieces[n + 1]) if n + 1 < len(pieces) else None
            ext = r[PROJ_HALO - SUBLANES:, :] if rc == 0 else jnp.concatenate([tail, r], axis=0)
            tail = r[r.shape[0] - SUBLANES:, :]
            conv(cs, rc, ext)
            r = r_next

    is_silu = functools.reduce(lambda a, b: a | b, [j == b for b in _SILU_BLOCKS])

    @pl.when(is_silu)
    def _():
        half = _dot_nt(h_ref[PROJ_HALO:, :], (0.5 * w_ref[...]).astype(bf16))
        o_ref[...] = _silu_of_half(half).astype(bf16)

    @pl.when((j >= _CONV_BLOCKS) & jnp.logical_not(is_silu))
    def _():
        o_ref[...] = _dot_nt(h_ref[PROJ_HALO:, :], w_ref[...].astype(bf16)).astype(bf16)


def _inproj(x2, gain, w_all, w_small, cw, cb, seq):
    n_tok = x2.shape[0]
    tm = min(TM_PROJ, seq)
    grid = (n_tok // tm, _N_BLOCKS)
    halo_blocks = tm // PROJ_HALO

    def conv_block(i, j):
        return (0, jnp.minimum(j, _CONV_BLOCKS - 1))

    return pl.pallas_call(
        functools.partial(_inproj_kernel, tiles_per_seq=seq // tm),
        grid=grid,
        in_specs=[
            pl.BlockSpec((tm, D_MODEL), lambda i, j: (i, 0)),
            pl.BlockSpec((PROJ_HALO, D_MODEL), lambda i, j: (jnp.maximum(i * halo_blocks - 1, 0), 0)),
            pl.BlockSpec((1, D_MODEL), lambda i, j: (0, 0)),
            pl.BlockSpec((pl.Element(TN_PROJ), pl.Element(D_MODEL)), lambda i, j: _src_row(j)),
            pl.BlockSpec((LANES, D_MODEL), lambda i, j: (0, 0)),
            pl.BlockSpec((SSM_CONV, TN_PROJ), conv_block),
            pl.BlockSpec((1, TN_PROJ), conv_block),
        ],
        out_specs=[
            pl.BlockSpec((tm, TN_PROJ), lambda i, j: (i, jnp.maximum(j - _CONV_BLOCKS, 0))),
            pl.BlockSpec((tm, TN_PROJ), lambda i, j: (i, jnp.minimum(j, _CONV_BLOCKS - 1))),
            pl.BlockSpec((tm, LANES), lambda i, j: (i, 0)),
        ],
        out_shape=[
            jax.ShapeDtypeStruct((n_tok, MAIN_WIDTH), bf16),
            jax.ShapeDtypeStruct((n_tok, SSM_XBC), bf16),
            jax.ShapeDtypeStruct((n_tok, LANES), f32),
        ],
        scratch_shapes=[pltpu.VMEM((PROJ_HALO + tm, D_MODEL), bf16)],
        compiler_params=pltpu.CompilerParams(
            dimension_semantics=("arbitrary", "arbitrary"),
            vmem_limit_bytes=VMEM_LIMIT),
        name="inproj",
    )(x2, x2, gain, w_all, w_small, cw, cb)


def _gla_kernel(q_ref, k_ref, v_ref, go_ref, sm_ref, wg_ref, bg_ref, gn_ref,
                o_ref, st_ref, gl_ref):
    C = GLA_CHUNK

    @pl.when(pl.program_id(2) == 0)
    def _():
        st_ref[...] = jnp.zeros_like(st_ref)

    a_lr = sm_ref[:, SM_ALR:SM_ALR + GLA_GATE_RANK].astype(bf16)
    logit = _dot(a_lr, wg_ref[...].astype(bf16)) + bg_ref[...]
    gl_ref[...] = (jnp.minimum(logit, 0.0)
                   - jnp.log(1.0 + jnp.exp(-jnp.abs(logit)))) * (1.0 / GLA_GATE_NORM)

    tri, causal = _tri(C, bf16)
    gain = gn_ref[...]
    n_chunks = q_ref.shape[0] // C

    chunks = [slice(c * C, (c + 1) * C) for c in range(n_chunks)]
    b3s = []
    for rows in chunks:
        hi, mid, lo = _split3(gl_ref[rows, :])
        b3s.append(_dot(tri, jnp.concatenate([hi, mid, lo], axis=1)))
    q_tb, k_tb, k_db, decay = [], [], [], []
    for rows, b3 in zip(chunks, b3s):
        b = (b3[:, 0:GLA_DK] + b3[:, GLA_DK:2 * GLA_DK]) + b3[:, 2 * GLA_DK:3 * GLA_DK]
        b_last = b[C - 1:C, :]
        kf = k_ref[rows, :].astype(f32)
        q_tb.append(((q_ref[rows, :].astype(f32) * (GLA_DK ** -0.5)) * jnp.exp(b)).astype(bf16))
        k_tb.append((kf * jnp.exp(-b)).astype(bf16))
        k_db.append((kf * jnp.exp(b_last - b)).astype(bf16))
        decay.append(jnp.exp(b_last))
    att = [_dot_nt(q_tb[c], k_tb[c]) for c in range(n_chunks)]
    upd = [_dot_tn(v_ref[chunks[c], :], k_db[c]) for c in range(n_chunks)]
    o_intra = [_dot(jnp.where(causal, att[c], 0.0).astype(bf16), v_ref[chunks[c], :])
               for c in range(n_chunks)]
    st = st_ref[...]
    for c, rows in enumerate(chunks):
        o = o_intra[c] + _dot_nt(q_tb[c], st.astype(bf16))
        st = decay[c] * st + upd[c]
        o = _rms(o, gain) * go_ref[rows, :].astype(f32)
        o_ref[rows, :] = o.astype(bf16)
    st_ref[...] = st


def _gla(proj, small, w_gate, b_gate, gla_norm, bsz, seq):
    n_tok = bsz * seq
    tb = min(TB_GLA, seq)
    n_t = seq // tb

    def row(b, h, t):
        return b * n_t + t

    return pl.pallas_call(
        _gla_kernel,
        grid=(bsz, GLA_HEADS, n_t),
        in_specs=[
            pl.BlockSpec((tb, GLA_DK), lambda b, h, t: (row(b, h, t), OFF_Q // GLA_DK + h)),
            pl.BlockSpec((tb, GLA_DK), lambda b, h, t: (row(b, h, t), OFF_K // GLA_DK + h)),
            pl.BlockSpec((tb, GLA_DV), lambda b, h, t: (row(b, h, t), OFF_V // GLA_DV + h)),
            pl.BlockSpec((tb, GLA_DV), lambda b, h, t: (row(b, h, t), OFF_GO // GLA_DV + h)),
            pl.BlockSpec((tb, LANES), lambda b, h, t: (row(b, h, t), 0)),
            pl.BlockSpec((GLA_GATE_RANK, GLA_DK), lambda b, h, t: (0, h)),
            pl.BlockSpec((1, GLA_DK), lambda b, h, t: (0, h)),
            pl.BlockSpec((1, GLA_DV), lambda b, h, t: (0, 0)),
        ],
        out_specs=pl.BlockSpec((tb, GLA_DV), lambda b, h, t: (row(b, h, t), h)),
        out_shape=jax.ShapeDtypeStruct((n_tok, GLA_VAL), bf16),
        scratch_shapes=[pltpu.VMEM((GLA_DV, GLA_DK), f32),
                        pltpu.VMEM((tb, GLA_DK), f32)],
        compiler_params=pltpu.CompilerParams(
            dimension_semantics=("arbitrary", "arbitrary", "arbitrary"),
            vmem_limit_bytes=VMEM_LIMIT),
        name="gla",
    )(proj, proj, proj, proj, small, w_gate, b_gate, gla_norm)


def _ssd_kernel(z_ref, xc_ref, sm_ref, dtb_ref, arow_ref, dsk_ref, nrm_ref, e_ref,
                o_ref, st_ref):
    tb = xc_ref.shape[0]
    L = SSM_CHUNK

    @pl.when(pl.program_id(1) == 0)
    def _():
        st_ref[...] = jnp.zeros_like(st_ref)

    tri, causal = _tri(L, bf16)
    lane = lax.broadcasted_iota(jnp.int32, (L, LANES), 1)
    left = lane < SSM_HEADDIM

    n_chunks = tb // L
    chunks = [slice(c * L, (c + 1) * L) for c in range(n_chunks)]
    groups = [slice(g * SSM_GROUP_WIDTH, (g + 1) * SSM_GROUP_WIDTH) for g in range(SSM_GROUPS)]
    CD = 2 * SUBLANES

    dts = [jax.nn.softplus(sm_ref[rows, :] + dtb_ref[...]) for rows in chunks]
    parts = []
    for dt in dts:
        parts.extend(_split3(dt * arow_ref[...]))
    cs_all = _dot(tri, jnp.concatenate(parts, axis=1))
    a_cs, a_cs_t, lhs_main, lhs_cd = [], [], [], []
    for c, dt in enumerate(dts):
        o3 = 3 * LANES * c
        acs = (cs_all[:, o3:o3 + LANES] + cs_all[:, o3 + LANES:o3 + 2 * LANES]) \
            + cs_all[:, o3 + 2 * LANES:o3 + 3 * LANES]
        a_last = acs[L - 1:L, :]
        a_cs.append(acs)
        a_cs_t.append((acs - jnp.log(dt)).T)
        lhs_main += [(dt * jnp.exp(a_last - acs)).astype(bf16), jnp.exp(acs).astype(bf16)]
        lhs_cd += list(_split3(jnp.broadcast_to(jnp.exp(a_last), (CD, LANES))))
    lhs = jnp.concatenate(lhs_main + lhs_cd, axis=0)

    exp_g = [_dot(lhs, e_ref[g]) for g in range(SSM_GROUPS)]

    def expanded(g, c, k):
        r0 = (2 * c + k) * L
        return exp_g[g][r0:r0 + L, :]

    def chunk_decay(g, c):
        r0 = 2 * L * n_chunks + 3 * CD * c
        r = exp_g[g]
        return (r[r0:r0 + 1, :] + r[r0 + CD:r0 + CD + 1, :]) + r[r0 + 2 * CD:r0 + 2 * CD + 1, :]

    pairs = [(c, g) for c in range(n_chunks) for g in range(SSM_GROUPS)]
    bm, cm, xg, scores, upd = {}, {}, {}, {}, {}
    for c, g in pairs:
        rows = chunks[c]
        bm[c, g] = xc_ref[rows, SSM_INNER + g * SSM_STATE:SSM_INNER + (g + 1) * SSM_STATE]
        cm[c, g] = xc_ref[rows, SSM_INNER + SSM_BC + g * SSM_STATE:
                          SSM_INNER + SSM_BC + (g + 1) * SSM_STATE]
        xg[c, g] = xc_ref[rows, groups[g]].astype(f32)
        scores[c, g] = _dot_nt(cm[c, g], bm[c, g])
        upd[c, g] = _dot_tn(bm[c, g], (xg[c, g] * expanded(g, c, 0)).astype(bf16))

    y_diag = {}
    for c, g in pairs:
        tiles = []
        for qd in range(SSM_HPG // 2):
            xt = xc_ref[chunks[c], g * SSM_GROUP_WIDTH + qd * LANES:
                        g * SSM_GROUP_WIDTH + (qd + 1) * LANES]
            halves = []
            for j in (2 * qd, 2 * qd + 1):
                ln = SM_DT + g * SSM_HPG + j
                col = jnp.broadcast_to(a_cs[c][:, ln:ln + 1], (L, L))
                rw = jnp.broadcast_to(a_cs_t[c][ln:ln + 1, :], (L, L))
                ldec = jnp.exp(jnp.where(causal, col - rw, -jnp.inf))
                halves.append(_dot((scores[c, g] * ldec).astype(bf16), xt))
            tiles.append(jnp.where(left, halves[0], halves[1]))
        y_diag[c, g] = jnp.concatenate(tiles, axis=1)

    for g in range(SSM_GROUPS):
        st = st_ref[g]
        for c in range(n_chunks):
            rows = chunks[c]
            y = (y_diag[c, g] + _dot(cm[c, g], st.astype(bf16)) * expanded(g, c, 1)
                 + xg[c, g] * dsk_ref[:, groups[g]])
            st = st * chunk_decay(g, c) + upd[c, g]
            y = y * z_ref[rows, groups[g]].astype(f32)
            y = _rms(y, nrm_ref[:, groups[g]])
            o_ref[rows, groups[g]] = y.astype(bf16)
        st_ref[g] = st


def _ssd(proj, xbc, small, dtb_row, a_row, dsk_row, nrm, e_mat, bsz, seq):
    n_tok = bsz * seq
    tb = min(TB_SSD, seq)
    n_t = seq // tb

    def row(b, t):
        return b * n_t + t

    const2 = lambda b, t: (0, 0)
    return pl.pallas_call(
        _ssd_kernel,
        grid=(bsz, n_t),
        in_specs=[
            pl.BlockSpec((tb, SSM_INNER), lambda b, t: (row(b, t), OFF_Z // SSM_INNER)),
            pl.BlockSpec((tb, SSM_XBC), lambda b, t: (row(b, t), 0)),
            pl.BlockSpec((tb, LANES), lambda b, t: (row(b, t), 0)),
            pl.BlockSpec((1, LANES), const2),
            pl.BlockSpec((1, LANES), const2),
            pl.BlockSpec((1, SSM_INNER), const2),
            pl.BlockSpec((1, SSM_INNER), const2),
            pl.BlockSpec((SSM_GROUPS, LANES, SSM_GROUP_WIDTH), lambda b, t: (0, 0, 0)),
        ],
        out_specs=pl.BlockSpec((tb, SSM_INNER), lambda b, t: (row(b, t), 0)),
        out_shape=jax.ShapeDtypeStruct((n_tok, SSM_INNER), bf16),
        scratch_shapes=[pltpu.VMEM((SSM_GROUPS, SSM_STATE, SSM_GROUP_WIDTH), f32)],
        compiler_params=pltpu.CompilerParams(
            dimension_semantics=("arbitrary", "arbitrary"),
            vmem_limit_bytes=VMEM_LIMIT),
        name="ssd",
    )(proj, xbc, small, dtb_row, a_row, dsk_row, nrm, e_mat)


def _merge_kernel(oa_ref, ob_ref, ga_ref, gb_ref, x_ref, wa_ref, wb_ref, wo_ref, o_ref):
    ma = _dot(oa_ref[...], wa_ref[...])
    mb = _dot(ob_ref[...], wb_ref[...])
    merged = (jax.nn.sigmoid(ga_ref[...].astype(f32)) * ma
              + jax.nn.sigmoid(gb_ref[...].astype(f32)) * mb)
    o_ref[...] = x_ref[...] + _dot(merged.astype(bf16), wo_ref[...])


def _resident(shape):
    nd = len(shape)
    return pl.BlockSpec(shape, lambda *_: (0,) * nd, pipeline_mode=pl.Buffered(1))


def _merge(o_a, o_b, proj, x2, wa, wb, wo):
    n_tok = x2.shape[0]
    tm = min(TM_MERGE, n_tok)
    return pl.pallas_call(
        _merge_kernel,
        grid=(n_tok // tm,),
        in_specs=[
            pl.BlockSpec((tm, GLA_VAL), lambda i: (i, 0)),
            pl.BlockSpec((tm, SSM_INNER), lambda i: (i, 0)),
            pl.BlockSpec((tm, D_MODEL), lambda i: (i, OFF_GA // D_MODEL)),
            pl.BlockSpec((tm, D_MODEL), lambda i: (i, OFF_GB // D_MODEL)),
            pl.BlockSpec((tm, D_MODEL), lambda i: (i, 0)),
            _resident((GLA_VAL, D_MODEL)),
            _resident((SSM_INNER, D_MODEL)),
            _resident((D_MODEL, D_MODEL)),
        ],
        out_specs=pl.BlockSpec((tm, D_MODEL), lambda i: (i, 0)),
        out_shape=jax.ShapeDtypeStruct((n_tok, D_MODEL), f32),
        compiler_params=pltpu.CompilerParams(
            dimension_semantics=("arbitrary",),
            vmem_limit_bytes=VMEM_LIMIT),
        name="merge",
    )(o_a, o_b, proj, proj, x2, wa, wb, wo)


def _ffn_kernel(x_ref, p_ref, fn_ref, wup_ref, cw_ref, cb_ref, wdn_ref, pn_ref,
                wpg_ref, wpp_ref, fin_ref, o_ref, g_ref, halo_ref, ae_ref):
    tm = x_ref.shape[0]
    H = SUBLANES

    @pl.when(pl.program_id(1) == 0)
    def _():
        halo_ref[...] = jnp.zeros_like(halo_ref)

    x1 = x_ref[...]
    h = _rms(x1, fn_ref[...]).astype(bf16)
    for c in range(FFN_HIDDEN // FFN_CB):
        cols = slice(c * FFN_CB, (c + 1) * FFN_CB)
        act = _dot(h, wup_ref[:, cols])
        lin = _dot(h, wup_ref[:, FFN_HIDDEN + c * FFN_CB:FFN_HIDDEN + (c + 1) * FFN_CB])
        ae_ref[0:H, :] = halo_ref[:, cols]
        ae_ref[H:H + tm, :] = act
        halo_ref[:, cols] = act[tm - H:tm, :]
        conv = cb_ref[:, cols]
        for k in range(FFN_CONV):
            conv = conv + cw_ref[k:k + 1, cols] * ae_ref[pl.ds(H - (FFN_CONV - 1) + k, tm), :]
        g_ref[:, cols] = (jax.nn.gelu(conv) * lin).astype(bf16)

    x2 = x1 + _dot(g_ref[...], wdn_ref[...])
    gate = jax.nn.sigmoid(_dot(_rms(x2, pn_ref[...]).astype(bf16), wpg_ref[...]))
    x3 = x2 + gate * _dot(p_ref[...].astype(bf16), wpp_ref[...])
    o_ref[...] = _rms(x3, fin_ref[...])


def _ffn(x1, p2, ffn_norm, wup, cw, cb, wdn, ple_norm, wpg, wpp, final_norm, bsz, seq):
    n_tok = bsz * seq
    tm = min(TM_FFN, seq)
    n_t = seq // tm
    rowmap = lambda b, t: (b * n_t + t, 0)
    return pl.pallas_call(
        _ffn_kernel,
        grid=(bsz, n_t),
        in_specs=[
            pl.BlockSpec((tm, D_MODEL), rowmap),
            pl.BlockSpec((tm, PLE_DIM), rowmap),
            _resident((1, D_MODEL)),
            _resident((D_MODEL, 2 * FFN_HIDDEN)),
            _resident((FFN_CONV, FFN_HIDDEN)),
            _resident((1, FFN_HIDDEN)),
            _resident((FFN_HIDDEN, D_MODEL)),
            _resident((1, D_MODEL)),
            _resident((D_MODEL, D_MODEL)),
            _resident((PLE_DIM, D_MODEL)),
            _resident((1, D_MODEL)),
        ],
        out_specs=pl.BlockSpec((tm, D_MODEL), rowmap),
        out_shape=jax.ShapeDtypeStruct((n_tok, D_MODEL), f32),
        scratch_shapes=[
            pltpu.VMEM((tm, FFN_HIDDEN), bf16),
            pltpu.VMEM((SUBLANES, FFN_HIDDEN), f32),
            pltpu.VMEM((tm + SUBLANES, FFN_CB), f32),
        ],
        compiler_params=pltpu.CompilerParams(
            dimension_semantics=("arbitrary", "arbitrary"),
            vmem_limit_bytes=VMEM_LIMIT),
        name="ffn",
    )(x1, p2, ffn_norm, wup, cw, cb, wdn, ple_norm, wpg, wpp, final_norm)


_IN_SIZES = (GLA_KEY, GLA_KEY, GLA_VAL, GLA_VAL, GLA_GATE_RANK, SSM_INNER, SSM_XBC, SSM_HEADS,
             D_MODEL, D_MODEL)
_IN_OFFS = tuple(sum(_IN_SIZES[:n]) for n in range(len(_IN_SIZES) + 1))
_IN_WIDTH = _IN_OFFS[-1]
_REPACK_SRC = (tuple(_IN_OFFS[6] + TN_PROJ * n for n in range(SSM_XBC // TN_PROJ))
               + tuple(_IN_OFFS[5] + TN_PROJ * n for n in range(SSM_INNER // TN_PROJ))
               + (_IN_OFFS[0],)
               + tuple(_IN_OFFS[2] + TN_PROJ * n for n in range(GLA_VAL // TN_PROJ))
               + tuple(_IN_OFFS[3] + TN_PROJ * n for n in range(GLA_VAL // TN_PROJ))
               + (_IN_OFFS[8], _IN_OFFS[9]))
assert len(_REPACK_SRC) == _N_BLOCKS and 2 * GLA_KEY == TN_PROJ
assert all(s % SUBLANES == 0 for s in _REPACK_SRC)


def _src_row(b):
    tile = functools.reduce(lambda acc, ns: jnp.where(b == ns[0], ns[1] // SUBLANES, acc),
                            list(enumerate(_REPACK_SRC)), 0)
    return (tile * SUBLANES, 0)


def _lane_row(vals, offset):
    return jnp.zeros((1, LANES), f32).at[0, offset:offset + vals.shape[0]].set(vals.astype(f32))


def _expand_mats():
    src = jnp.arange(LANES)[None, :, None]
    dst = jnp.arange(SSM_GROUP_WIDTH)[None, None, :]
    g = jnp.arange(SSM_GROUPS)[:, None, None]
    return (src == SM_DT + g * SSM_HPG + dst // SSM_HEADDIM).astype(bf16)


def _layer(x2, p2, bsz, seq, mixer_norm, w_in, w_gla_gate, b_gla_gate, gla_norm, w_ssm_conv,
           b_ssm_conv, dt_bias, a_log, d_skip, ssm_norm, w_branch_a, w_branch_b, w_out,
           ffn_norm, w_ffn_up, w_ffn_conv, b_ffn_conv, w_ffn_down, ple_norm, w_ple_gate,
           w_ple_proj, out_norm):
    assert w_in.shape == (D_MODEL, _IN_WIDTH)
    w_t = w_in.T
    w_small = jnp.zeros((LANES, D_MODEL), f32)
    w_small = w_small.at[SM_ALR:SM_ALR + GLA_GATE_RANK].set(w_t[_IN_OFFS[4]:_IN_OFFS[5]])
    w_small = w_small.at[SM_DT:SM_DT + SSM_HEADS].set(w_t[_IN_OFFS[7]:_IN_OFFS[8]]).astype(bf16)

    proj, xbc, small = _inproj(x2, mixer_norm[None, :], w_t, w_small, w_ssm_conv,
                               b_ssm_conv[None, :], seq)

    o_a = _gla(proj, small, w_gla_gate, b_gla_gate[None, :], gla_norm[None, :], bsz, seq)

    o_b = _ssd(proj, xbc, small,
               _lane_row(dt_bias, SM_DT), _lane_row(-jnp.exp(a_log.astype(f32)), SM_DT),
               jnp.repeat(d_skip.astype(f32), SSM_HEADDIM)[None, :], ssm_norm[None, :],
               _expand_mats(), bsz, seq)

    x1 = _merge(o_a, o_b, proj, x2, w_branch_a.astype(bf16), w_branch_b.astype(bf16),
                w_out.astype(bf16))

    return _ffn(x1, p2, ffn_norm[None, :], w_ffn_up.astype(bf16), w_ffn_conv, b_ffn_conv[None, :],
                w_ffn_down.astype(bf16), ple_norm[None, :], w_ple_gate.astype(bf16),
                w_ple_proj.astype(bf16), out_norm[None, :], bsz, seq)


def kernel(x, p, mixer_norm, w_in, w_gla_gate, b_gla_gate, gla_norm, w_ssm_conv, b_ssm_conv,
           dt_bias, a_log, d_skip, ssm_norm, w_branch_a, w_branch_b, w_out, ffn_norm, w_ffn_up,
           w_ffn_conv, b_ffn_conv, w_ffn_down, ple_norm, w_ple_gate, w_ple_proj, final_norm):
    bsz, seq, _ = x.shape
    depth = w_in.shape[0]
    assert depth == 1, "the fused final RMSNorm assumes a single layer"
    x2 = x.reshape(bsz * seq, D_MODEL)
    out = _layer(x2, p[0].reshape(bsz * seq, PLE_DIM), bsz, seq, mixer_norm[0], w_in[0],
                 w_gla_gate[0], b_gla_gate[0], gla_norm[0], w_ssm_conv[0], b_ssm_conv[0],
                 dt_bias[0], a_log[0], d_skip[0], ssm_norm[0], w_branch_a[0], w_branch_b[0],
                 w_out[0], ffn_norm[0], w_ffn_up[0], w_ffn_conv[0], b_ffn_conv[0], w_ffn_down[0],
                 ple_norm[0], w_ple_gate[0], w_ple_proj[0], final_norm)
    return out.reshape(bsz, seq, D_MODEL)
```
